```python
import math
import jax, jax.numpy as jnp
from jax import lax
import numpy as np

D_MODEL = 1024
BATCH = 8
SEQ = 8192
DEPTH = 1
DEC_BATCH = 128
DEC_SEQ = 4
PAST_LEN = 8192
PAGE_SIZE = 128

H_DIFF = 8
D_QK = D_MODEL // 16
D_V = 2 * D_QK
CONV_CH = D_MODEL
CONV_W = 31
D_FF = ((8 * D_MODEL // 3 + 127) // 128) * 128
X_HEADS = 4
X_DH = D_MODEL // 8
N_MEM = 256
N_BUCKETS = 32
MAX_DIST = 128
Q_BLOCK = 128
EPS = 1e-6
NEG = -1e30
QK_SCALE = D_QK ** -0.5
Q_COLS = H_DIFF * 2 * D_QK
K_COLS = H_DIFF * 2 * D_QK
V_COLS = H_DIFF * D_V
GLU_COLS = 2 * CONV_CH
GATE_COLS = 2 * D_MODEL
IN_COLS = Q_COLS + K_COLS + V_COLS + GLU_COLS + GATE_COLS

kernel_name = "diffattn_conformer_gated_hybrid_step"

F32 = jnp.float32


def rmsnorm(x, g):
    xf = x.astype(F32)
    xf = xf * lax.rsqrt(jnp.mean(xf * xf, axis=-1, keepdims=True) + EPS)
    return (xf * g.astype(F32)).astype(x.dtype)


def layernorm(x, g, b):
    xf = x.astype(F32)
    mu = jnp.mean(xf, axis=-1, keepdims=True)
    xc = xf - mu
    var = jnp.mean(xc * xc, axis=-1, keepdims=True)
    return (xc * lax.rsqrt(var + EPS) * g.astype(F32) + b.astype(F32)).astype(x.dtype)


def swiglu_ffn(x, w_in, w_out):
    gate, up = jnp.split(x @ w_in, 2, axis=-1)
    return (jax.nn.silu(gate) * up) @ w_out


def t5_bucket(rel):
    n = jnp.maximum(rel, 0)
    max_exact = N_BUCKETS // 2
    log_ratio = jnp.log(jnp.maximum(n, 1).astype(F32) / max_exact) / math.log(MAX_DIST / max_exact)
    large = jnp.minimum(max_exact + (log_ratio * (N_BUCKETS - max_exact)).astype(jnp.int32), N_BUCKETS - 1)
    return jnp.where(n < max_exact, n, large)


def rel_bias(q_pos, k_pos, table):
    b = table[t5_bucket(q_pos[:, None] - k_pos[None, :])]
    return jnp.moveaxis(b, -1, 0).astype(F32)


def attn_stats(q, k, v, bias, mask):
    s = jnp.einsum('bqhmd,bkhmd->bmhqk', q.astype(F32), k.astype(F32)) * QK_SCALE + bias
    s = jnp.where(mask, s, NEG)
    m = jnp.max(s, axis=-1)
    p = jnp.exp(s - m[..., None])
    l = jnp.sum(p, axis=-1)
    acc = jnp.einsum('bmhqk,bkhd->bmhqd', p, v.astype(F32))
    return (m, l, acc)


def merge_stats(a, b):
    ma, la, aa = a
    mb, lb, ab = b
    m = jnp.maximum(ma, mb)
    ca = jnp.exp(ma - m)
    cb = jnp.exp(mb - m)
    return (m, la * ca + lb * cb, aa * ca[..., None] + ab * cb[..., None])


def finalize(stats, lam):
    _, l, acc = stats
    o = acc / l[..., None]
    o = o[:, 0] - lam * o[:, 1]
    return jnp.transpose(o, (0, 2, 1, 3))


def diff_lambda(lq1, lk1, lq2, lk2, lam_init):
    return (jnp.exp(jnp.sum(lq1.astype(F32) * lk1.astype(F32)))
            - jnp.exp(jnp.sum(lq2.astype(F32) * lk2.astype(F32))) + lam_init)


def diff_attn_prompt(q, k, v, table, lam):
    b, t = q.shape[:2]
    nqb = t // Q_BLOCK
    qb = jnp.moveaxis(q.reshape(b, nqb, Q_BLOCK, H_DIFF, 2, D_QK), 1, 0)
    k_pos = jnp.arange(t)

    def one_block(args):
        i, qi = args
        q_pos = i * Q_BLOCK + jnp.arange(Q_BLOCK)
        bias = rel_bias(q_pos, k_pos, table)
        mask = k_pos[None, :] <= q_pos[:, None]
        return finalize(attn_stats(qi, k, v, bias, mask), lam)

    o = lax.map(one_block, (jnp.arange(nqb), qb))
    return jnp.moveaxis(o, 0, 1).reshape(b, t, H_DIFF, D_V)


def diff_attn_sample(q, k_new, v_new, cache_k, cache_v, page_table, table, lam):
    bd, tn = q.shape[:2]
    n_pages = page_table.shape[1]
    q_pos = n_pages * PAGE_SIZE + jnp.arange(tn)
    init = (jnp.full((bd, 2, H_DIFF, tn), NEG, F32),
            jnp.zeros((bd, 2, H_DIFF, tn), F32),
            jnp.zeros((bd, 2, H_DIFF, tn, D_V), F32))
    page_pos = jnp.arange(PAGE_SIZE)
    full_mask = jnp.ones((tn, PAGE_SIZE), dtype=bool)

    def step(carry, xs):
        j, phys = xs
        kp = cache_k[phys].reshape(bd, PAGE_SIZE, H_DIFF, 2, D_QK)
        vp = cache_v[phys]
        bias = rel_bias(q_pos, j * PAGE_SIZE + page_pos, table)
        return merge_stats(carry, attn_stats(q, kp, vp, bias, full_mask)), None

    stats, _ = lax.scan(step, init, (jnp.arange(n_pages), page_table.T))
    bias_new = rel_bias(q_pos, q_pos, table)
    causal = q_pos[None, :] <= q_pos[:, None]
    stats = merge_stats(stats, attn_stats(q, k_new, v_new, bias_new, causal))
    return finalize(stats, lam)


def diff_attn_out(o, subln, w_o, lam_init, dtype):
    of = o * lax.rsqrt(jnp.mean(o * o, axis=-1, keepdims=True) + EPS) * subln.astype(F32) * (1.0 - lam_init)
    return of.reshape(o.shape[0], o.shape[1], H_DIFF * D_V).astype(dtype) @ w_o


def mix_inputs(u, w_in):
    z = u @ w_in
    b, t = u.shape[:2]
    o1 = Q_COLS
    o2 = o1 + K_COLS
    o3 = o2 + V_COLS
    o4 = o3 + GLU_COLS
    q = z[..., :o1].reshape(b, t, H_DIFF, 2, D_QK)
    k = z[..., o1:o2].reshape(b, t, H_DIFF, 2 * D_QK)
    v = z[..., o2:o3].reshape(b, t, H_DIFF, D_V)
    glu = z[..., o3:o4]
    a = glu[..., :CONV_CH] * jax.nn.sigmoid(glu[..., CONV_CH:])
    gate_a = jax.nn.sigmoid(z[..., o4:o4 + D_MODEL])
    gate_c = jax.nn.sigmoid(z[..., o4 + D_MODEL:])
    return q, k, v, a, gate_a, gate_c


def conv_branch(glu_pad, dw_w, dw_b, ln_g, ln_b, w_o):
    y = lax.conv_general_dilated(glu_pad, dw_w[:, None, :], window_strides=(1,), padding='VALID',
                                 dimension_numbers=('NWC', 'WIO', 'NWC'),
                                 feature_group_count=CONV_CH) + dw_b
    y = jax.nn.silu(layernorm(y, ln_g, ln_b))
    return y @ w_o


def mem_kv(mem, g, w_xkv):
    b, n = mem.shape[:2]
    k, v = jnp.split(rmsnorm(mem, g) @ w_xkv, 2, axis=-1)
    return k.reshape(b, n, X_HEADS, X_DH), v.reshape(b, n, X_HEADS, X_DH)


def cross_attend(u, mk, mv, w_xq, w_xo):
    b, t = u.shape[:2]
    q = (u @ w_xq).reshape(b, t, X_HEADS, X_DH)
    s = jnp.einsum('bqhd,bkhd->bhqk', q.astype(F32), mk.astype(F32)) * (X_DH ** -0.5)
    p = jax.nn.softmax(s, axis=-1)
    o = jnp.einsum('bhqk,bkhd->bqhd', p, mv.astype(F32)).reshape(b, t, X_HEADS * X_DH)
    return o.astype(u.dtype) @ w_xo


def setup_inputs(seed: int = 0) -> dict:
    key = jax.random.key(seed)
    ks = iter(jax.random.split(key, 48))

    def nrm(shape, scale=1.0):
        return jax.random.normal(next(ks), shape, F32) * scale

    def gain(shape):
        return 1.0 + nrm(shape, 0.02)

    L, D = DEPTH, D_MODEL
    n_pages = PAST_LEN // PAGE_SIZE
    n_used = DEC_BATCH * n_pages
    n_pool = n_used + n_used // 4
    page_table = jax.random.permutation(next(ks), n_pool)[:n_used].reshape(DEC_BATCH, n_pages).astype(jnp.int32)
    return {
        "x_prompt": nrm((BATCH, SEQ, D)),
        "x_sample": nrm((DEC_BATCH, DEC_SEQ, D)),
        "mem_prompt": nrm((BATCH, N_MEM, D)),
        "cache_k": nrm((L, n_pool, PAGE_SIZE, H_DIFF, 2 * D_QK)),
        "cache_v": nrm((L, n_pool, PAGE_SIZE, H_DIFF, D_V)),
        "page_table": page_table,
        "state_conv": nrm((L, DEC_BATCH, CONV_W - 1, CONV_CH), 0.5),
        "cache_mem_k": nrm((L, DEC_BATCH, N_MEM, X_HEADS, X_DH)),
        "cache_mem_v": nrm((L, DEC_BATCH, N_MEM, X_HEADS, X_DH)),
        "rel_bias_table": nrm((N_BUCKETS, H_DIFF), 0.2),
        "norm_ffn1": gain((L, D)),
        "ffn1_w_in": nrm((L, D, 2 * D_FF), D ** -0.5),
        "ffn1_w_out": nrm((L, D_FF, D), D_FF ** -0.5),
        "norm_mix": gain((L, D)),
        "w_in": nrm((L, D, IN_COLS), D ** -0.5),
        "lambda_q1": nrm((L, D_QK), 0.1),
        "lambda_k1": nrm((L, D_QK), 0.1),
        "lambda_q2": nrm((L, D_QK), 0.1),
        "lambda_k2": nrm((L, D_QK), 0.1),
        "subln": gain((L, D_V)),
        "w_attn_o": nrm((L, H_DIFF * D_V, D), (H_DIFF * D_V) ** -0.5),
        "conv_dw_w": nrm((L, CONV_W, CONV_CH), CONV_W ** -0.5),
        "conv_dw_b": nrm((L, CONV_CH), 0.02),
        "conv_ln_g": gain((L, CONV_CH)),
        "conv_ln_b": nrm((L, CONV_CH), 0.02),
        "w_conv_o": nrm((L, CONV_CH, D), CONV_CH ** -0.5),
        "w_out": nrm((L, D, D), D ** -0.5),
        "norm_cross": gain((L, D)),
        "norm_mem": gain((L, D)),
        "w_xq": nrm((L, D, X_HEADS * X_DH), D ** -0.5),
        "w_xkv": nrm((L, D, 2 * X_HEADS * X_DH), D ** -0.5),
        "w_xo": nrm((L, X_HEADS * X_DH, D), (X_HEADS * X_DH) ** -0.5),
        "norm_ffn2": gain((L, D)),
        "ffn2_w_in": nrm((L, D, 2 * D_FF), D ** -0.5),
        "ffn2_w_out": nrm((L, D_FF, D), D_FF ** -0.5),
        "norm_final": gain((D,)),
    }


def reference(x_prompt, x_sample, mem_prompt, cache_k, cache_v, page_table, state_conv,
              cache_mem_k, cache_mem_v, rel_bias_table, norm_ffn1, ffn1_w_in, ffn1_w_out,
              norm_mix, w_in, lambda_q1, lambda_k1, lambda_q2, lambda_k2, subln, w_attn_o,
              conv_dw_w, conv_dw_b, conv_ln_g, conv_ln_b, w_conv_o, w_out, norm_cross, norm_mem,
              w_xq, w_xkv, w_xo, norm_ffn2, ffn2_w_in, ffn2_w_out, norm_final):
    xp, xs = x_prompt, x_sample
    kp_l, vp_l, cp_l, mkp_l, mvp_l, ks_l, vs_l, cs_l = [], [], [], [], [], [], [], []
    for l in range(DEPTH):
        lam_init = 0.8 - 0.6 * math.exp(-0.3 * l)
        lam = diff_lambda(lambda_q1[l], lambda_k1[l], lambda_q2[l], lambda_k2[l], lam_init)

        xp = xp + 0.5 * swiglu_ffn(rmsnorm(xp, norm_ffn1[l]), ffn1_w_in[l], ffn1_w_out[l])
        xs = xs + 0.5 * swiglu_ffn(rmsnorm(xs, norm_ffn1[l]), ffn1_w_in[l], ffn1_w_out[l])

        qp, kp, vp, ap, gap, gcp = mix_inputs(rmsnorm(xp, norm_mix[l]), w_in[l])
        b, t = xp.shape[:2]
        att_p = diff_attn_prompt(qp, kp.reshape(b, t, H_DIFF, 2, D_QK), vp, rel_bias_table, lam)
        att_p = diff_attn_out(att_p, subln[l], w_attn_o[l], lam_init, xp.dtype)
        pad_p = jnp.concatenate([jnp.zeros((b, CONV_W - 1, CONV_CH), ap.dtype), ap], axis=1)
        cnv_p = conv_branch(pad_p, conv_dw_w[l], conv_dw_b[l], conv_ln_g[l], conv_ln_b[l], w_conv_o[l])
        xp = xp + (gap * att_p + gcp * cnv_p) @ w_out[l]

        qs, ks_, vs_, as_, gas, gcs = mix_inputs(rmsnorm(xs, norm_mix[l]), w_in[l])
        bd, tn = xs.shape[:2]
        att_s = diff_attn_sample(qs, ks_.reshape(bd, tn, H_DIFF, 2, D_QK), vs_, cache_k[l], cache_v[l],
                                 page_table, rel_bias_table, lam)
        att_s = diff_attn_out(att_s, subln[l], w_attn_o[l], lam_init, xs.dtype)
        pad_s = jnp.concatenate([state_conv[l].astype(as_.dtype), as_], axis=1)
        cnv_s = conv_branch(pad_s, conv_dw_w[l], conv_dw_b[l], conv_ln_g[l], conv_ln_b[l], w_conv_o[l])
        xs = xs + (gas * att_s + gcs * cnv_s) @ w_out[l]

        mkp, mvp = mem_kv(mem_prompt, norm_mem[l], w_xkv[l])
        xp = xp + cross_attend(rmsnorm(xp, norm_cross[l]), mkp, mvp, w_xq[l], w_xo[l])
        xs = xs + cross_attend(rmsnorm(xs, norm_cross[l]), cache_mem_k[l], cache_mem_v[l], w_xq[l], w_xo[l])

        xp = xp + 0.5 * swiglu_ffn(rmsnorm(xp, norm_ffn2[l]), ffn2_w_in[l], ffn2_w_out[l])
        xs = xs + 0.5 * swiglu_ffn(rmsnorm(xs, norm_ffn2[l]), ffn2_w_in[l], ffn2_w_out[l])

        kp_l.append(kp)
        vp_l.append(vp)
        cp_l.append(ap[:, -(CONV_W - 1):])
        mkp_l.append(mkp)
        mvp_l.append(mvp)
        ks_l.append(ks_)
        vs_l.append(vs_)
        cs_l.append(pad_s[:, -(CONV_W - 1):])

    y_prompt = rmsnorm(xp, norm_final)
    y_sample = rmsnorm(xs, norm_final)
    new_k_prompt = jnp.stack(kp_l, 0)
    new_v_prompt = jnp.stack(vp_l, 0)
    new_conv_prompt = jnp.stack(cp_l, 0)
    new_mem_k_prompt = jnp.stack(mkp_l, 0)
    new_mem_v_prompt = jnp.stack(mvp_l, 0)
    new_k_sample = jnp.stack(ks_l, 0)
    new_v_sample = jnp.stack(vs_l, 0)
    new_conv_sample = jnp.stack(cs_l, 0)
    return (y_prompt, y_sample, new_k_prompt, new_v_prompt, new_conv_prompt,
            new_mem_k_prompt, new_mem_v_prompt, new_k_sample, new_v_sample, new_conv_sample)
```

```python
import functools
import math

import jax
import jax.numpy as jnp
from jax import lax
from jax.experimental import pallas as pl
from jax.experimental.pallas import tpu as pltpu

F32 = jnp.float32
BF16 = jnp.bfloat16

D_MODEL = 1024
H_DIFF = 8
D_QK = 64
D_V = 128
HEAD_W = 2 * D_QK
CONV_CH = D_MODEL
CONV_W = 31
D_FF = 2816
X_HEADS = 4
X_DH = 128
N_BUCKETS = 32
MAX_DIST = 128
PAGE_SIZE = 128
EPS = 1e-6
NEG = -1e30
QK_SCALE = D_QK ** -0.5
X_SCALE = X_DH ** -0.5

V7X_VMEM_BYTES = 64 * 1024 * 1024
VMEM_LIMIT = 56 * 1024 * 1024

ROW_TILE = 256
ATT_BLK = 256
CONV_TILE = 256
CONV_HALO = 32
CONV_ROWS = 32
PAGES_PER_STEP = 8
NEW_PAD = 16


def _params(sem):
    return pltpu.CompilerParams(dimension_semantics=sem, vmem_limit_bytes=VMEM_LIMIT)


def _resident(shape):
    nd = len(shape)
    return pl.BlockSpec(shape, lambda *_: (0,) * nd, pipeline_mode=pl.Buffered(1))


def _rms(x, g):
    return x * lax.rsqrt(jnp.mean(x * x, axis=-1, keepdims=True) + EPS) * g


def _dot(a, b):
    return jnp.dot(a, b, preferred_element_type=F32)


def _dot_nt(a, b):
    return lax.dot_general(a, b, (((1,), (1,)), ((), ())), preferred_element_type=F32)


def _ffn_kernel(x_ref, g_ref, wg_ref, wu_ref, wo_ref, gf_ref, o_ref, *, final_norm):
    x = x_ref[...]
    u = _rms(x, g_ref[...]).astype(BF16)
    gate = _dot(u, wg_ref[...])
    up = _dot(u, wu_ref[...])
    act = (gate * jax.nn.sigmoid(gate) * up).astype(BF16)
    h = x + 0.5 * _dot(act, wo_ref[...])
    if final_norm:
        h = _rms(h, gf_ref[...])
    o_ref[...] = h


def _row_tile(n):
    tile = min(ROW_TILE, n)
    assert n % tile == 0 and tile % 8 == 0
    return tile


def _ffn(x, g, wg, wu, wo, gf, final_norm):
    n = x.shape[0]
    tile = _row_tile(n)
    row = pl.BlockSpec((tile, D_MODEL), lambda i: (i, 0))
    return pl.pallas_call(
        functools.partial(_ffn_kernel, final_norm=final_norm),
        out_shape=jax.ShapeDtypeStruct((n, D_MODEL), F32),
        grid=(n // tile,),
        in_specs=[row, _resident(g.shape), _resident(wg.shape), _resident(wu.shape),
                  _resident(wo.shape), _resident(gf.shape)],
        out_specs=row,
        compiler_params=_params(("parallel",)),
        name="ffn",
    )(x, g, wg, wu, wo, gf)


def _mixproj_kernel(h_ref, g_ref, w_ref, q_ref, k_ref, v_ref, kb_ref, vb_ref, a_ref, ga_ref, gc_ref):
    u = _rms(h_ref[...], g_ref[...]).astype(BF16)
    d = D_MODEL

    def cols(j):
        return _dot(u, w_ref[:, j * d:(j + 1) * d])

    q_ref[...] = (cols(0) * QK_SCALE).astype(BF16)
    k = cols(1)
    k_ref[...] = k
    kb_ref[...] = k.astype(BF16)
    v = cols(2)
    v_ref[...] = v
    vb_ref[...] = v.astype(BF16)
    a_ref[...] = cols(3) * jax.nn.sigmoid(cols(4))
    ga_ref[...] = jax.nn.sigmoid(cols(5))
    gc_ref[...] = jax.nn.sigmoid(cols(6))


def _mixproj(h, g, w):
    n = h.shape[0]
    tile = _row_tile(n)
    row = pl.BlockSpec((tile, D_MODEL), lambda i: (i, 0))
    f32o = jax.ShapeDtypeStruct((n, D_MODEL), F32)
    b16o = jax.ShapeDtypeStruct((n, D_MODEL), BF16)
    return pl.pallas_call(
        _mixproj_kernel,
        out_shape=(b16o, f32o, f32o, b16o, b16o, f32o, f32o, f32o),
        grid=(n // tile,),
        in_specs=[row, _resident(g.shape), _resident(w.shape)],
        out_specs=(row,) * 8,
        compiler_params=_params(("parallel",)),
        name="mixproj",
    )(h, g, w)


def _lambda_value(lamv_ref, lam_init):
    lv = lamv_ref[...]
    d1 = jnp.sum(lv[0:1] * lv[1:2], axis=-1, keepdims=True)
    d2 = jnp.sum(lv[2:3] * lv[3:4], axis=-1, keepdims=True)
    return jnp.exp(d1) - jnp.exp(d2) + lam_init


def _head_out(o1, o2, lam, subln, lam_init):
    o = o1 - lam * o2
    return o * lax.rsqrt(jnp.mean(o * o, axis=-1, keepdims=True) + EPS) * subln * (1.0 - lam_init)


def _softmax_step(carry, s, v):
    m, l, acc = carry
    m_new = jnp.maximum(m, jnp.max(s, axis=-1, keepdims=True))
    alpha = jnp.exp(m - m_new)
    p = jnp.exp(s - m_new)
    l = alpha * l + jnp.sum(p, axis=-1, keepdims=True)
    acc = alpha * acc + _dot(p.astype(BF16), v)
    return m_new, l, acc


def _pattn_kernel(q_ref, k_ref, v_ref, bias_ref, lamv_ref, subln_ref, o_ref, *, lam_init):
    i = pl.program_id(2)
    blk = ATT_BLK
    q = q_ref[0]
    lane = lax.broadcasted_iota(jnp.int32, q.shape, 1)
    zero = jnp.zeros_like(q)
    qq = jnp.concatenate([jnp.where(lane < D_QK, q, zero), jnp.where(lane >= D_QK, q, zero)], axis=0)

    def scores(j):
        start = pl.multiple_of(j * blk, blk)
        kj = k_ref[0, pl.ds(start, blk), :]
        vj = v_ref[0, pl.ds(start, blk), :]
        return _dot_nt(qq, kj), vj

    def far_step(j, carry):
        s, vj = scores(j)
        return _softmax_step(carry, s, vj)

    def biased_step(which):
        def step(j, carry):
            s, vj = scores(j)
            b = bias_ref[0, which]
            s = s + jnp.concatenate([b, b], axis=0)
            return _softmax_step(carry, s, vj)
        return step

    init = (jnp.full((2 * blk, 1), NEG, F32), jnp.zeros((2 * blk, 1), F32),
            jnp.zeros((2 * blk, D_V), F32))
    carry = lax.fori_loop(0, jnp.maximum(i - 1, 0), far_step, init)
    carry = lax.fori_loop(jnp.maximum(i - 1, 0), i, biased_step(1), carry)
    _, l, acc = biased_step(0)(i, carry)
    o = acc / l
    lam = _lambda_value(lamv_ref, lam_init)
    o_ref[0] = _head_out(o[:blk], o[blk:], lam, subln_ref[...], lam_init).astype(o_ref.dtype)


def _prompt_attention(q, kb, vb, bias_tiles, lamv, subln, lam_init):
    b, t, _ = q.shape
    blk = ATT_BLK
    qo = pl.BlockSpec((1, blk, HEAD_W), lambda bi, h, i: (bi, i, h))
    kv = pl.BlockSpec((1, t, HEAD_W), lambda bi, h, i: (bi, 0, h))
    return pl.pallas_call(
        functools.partial(_pattn_kernel, lam_init=lam_init),
        out_shape=jax.ShapeDtypeStruct((b, t, H_DIFF * D_V), BF16),
        grid=(b, H_DIFF, t // blk),
        in_specs=[qo, kv, kv,
                  pl.BlockSpec((1, 2, blk, blk), lambda bi, h, i: (h, 0, 0, 0)),
                  pl.BlockSpec(lamv.shape, lambda bi, h, i: (0, 0)),
                  pl.BlockSpec(subln.shape, lambda bi, h, i: (0, 0))],
        out_specs=qo,
        compiler_params=_params(("parallel", "parallel", "arbitrary")),
        name="prompt_attention",
    )(q, kb, vb, bias_tiles, lamv, subln)


def _sattn_kernel(pt_ref, q_ref, *refs, lam_init, n_steps):
    del pt_ref
    npg = PAGES_PER_STEP
    k_refs, v_refs = refs[:npg], refs[npg:2 * npg]
    (kn_ref, vn_ref, blast_ref, bnew_ref, lamv_ref, subln_ref, o_ref, m_ref, l_ref, acc_ref) = refs[2 * npg:]
    c = pl.program_id(1)

    @pl.when(c == 0)
    def _():
        m_ref[...] = jnp.full(m_ref.shape, NEG, F32)
        l_ref[...] = jnp.zeros(l_ref.shape, F32)
        acc_ref[...] = jnp.zeros(acc_ref.shape, F32)

    q = q_ref[0]
    s_pages = [_dot_nt(q, k_refs[p][0].astype(BF16)) for p in range(npg)]
    is_last = (c == n_steps - 1).astype(F32)
    s_pages[-1] = s_pages[-1] + is_last * blast_ref[...]
    s = jnp.concatenate(s_pages, axis=1)
    m = m_ref[...]
    m_new = jnp.maximum(m, jnp.max(s, axis=-1, keepdims=True))
    alpha = jnp.exp(m - m_new)
    p = jnp.exp(s - m_new)
    l_ref[...] = alpha * l_ref[...] + jnp.sum(p, axis=-1, keepdims=True)
    pb = p.astype(BF16)
    pv = _dot(pb[:, :PAGE_SIZE], v_refs[0][0].astype(BF16))
    for pg in range(1, npg):
        pv = pv + _dot(pb[:, pg * PAGE_SIZE:(pg + 1) * PAGE_SIZE], v_refs[pg][0].astype(BF16))
    acc_ref[...] = alpha * acc_ref[...] + pv
    m_ref[...] = m_new

    @pl.when(c == n_steps - 1)
    def _():
        s_new = _dot_nt(q, kn_ref[0]) + bnew_ref[...]
        m2, l2, acc2 = _softmax_step((m_ref[...], l_ref[...], acc_ref[...]), s_new, vn_ref[0])
        lam = _lambda_value(lamv_ref, lam_init)
        rows = acc2.shape[0] // H_DIFF
        nq = rows // 2
        for h in range(H_DIFF):
            o = acc2[h * rows:(h + 1) * rows, h * D_V:(h + 1) * D_V] / l2[h * rows:(h + 1) * rows]
            o_ref[0, :, h * D_V:(h + 1) * D_V] = _head_out(o[:nq], o[nq:], lam, subln_ref[...], lam_init
                                                           ).astype(o_ref.dtype)


def _sample_attention(q_rows, cache_k, cache_v, page_table, k_new, v_new, bias_last, bias_new, lamv, subln,
                      lam_init, nq):
    bd, n_pages = page_table.shape
    npg = PAGES_PER_STEP
    n_steps = n_pages // npg
    rows = q_rows.shape[1]
    width = H_DIFF * HEAD_W

    def page_spec(p):
        return pl.BlockSpec((1, PAGE_SIZE, width), lambda b, c, pt: (pt[b, c * npg + p], 0, 0))

    per_batch = lambda shape: pl.BlockSpec((1,) + shape, lambda b, c, pt: (b, 0, 0))
    const = lambda shape: pl.BlockSpec(shape, lambda b, c, pt: (0, 0))
    grid_spec = pltpu.PrefetchScalarGridSpec(
        num_scalar_prefetch=1,
        grid=(bd, n_steps),
        in_specs=[per_batch((rows, width))]
        + [page_spec(p) for p in range(npg)] * 2
        + [per_batch((NEW_PAD, width)), per_batch((NEW_PAD, width)),
           const(bias_last.shape), const(bias_new.shape), const(lamv.shape), const(subln.shape)],
        out_specs=per_batch((nq, width)),
        scratch_shapes=[pltpu.VMEM((rows, 1), F32), pltpu.VMEM((rows, 1), F32), pltpu.VMEM((rows, width), F32)],
    )
    return pl.pallas_call(
        functools.partial(_sattn_kernel, lam_init=lam_init, n_steps=n_steps),
        out_shape=jax.ShapeDtypeStruct((bd, nq, width), BF16),
        grid_spec=grid_spec,
        compiler_params=_params(("parallel", "arbitrary")),
        name="sample_attention",
    )(page_table, q_rows, *([cache_k] * npg), *([cache_v] * npg), k_new, v_new, bias_last, bias_new, lamv, subln)


def _ln_silu(y, g, b):
    mu = jnp.mean(y, axis=-1, keepdims=True)
    yc = y - mu
    var = jnp.mean(yc * yc, axis=-1, keepdims=True)
    z = yc * lax.rsqrt(var + EPS) * g + b
    return z * jax.nn.sigmoid(z)


def _conv_prompt_kernel(cur_ref, halo_ref, dw_ref, db_ref, lg_ref, lb_ref, o_ref, win_ref):
    i = pl.program_id(1)
    keep = (i > 0).astype(F32)
    win_ref[0:CONV_HALO, :] = halo_ref[0] * keep
    win_ref[CONV_HALO:, :] = cur_ref[0]
    first = CONV_HALO - (CONV_W - 1)

    for r0 in range(0, CONV_TILE, CONV_ROWS):
        acc = win_ref[r0 + first:r0 + first + CONV_ROWS, :] * dw_ref[0:1, :]
        for w in range(1, CONV_W):
            acc = acc + win_ref[r0 + first + w:r0 + first + w + CONV_ROWS, :] * dw_ref[w:w + 1, :]
        y = _ln_silu(acc + db_ref[...], lg_ref[...], lb_ref[...])
        o_ref[0, r0:r0 + CONV_ROWS, :] = y.astype(o_ref.dtype)


def _conv_prompt(a, dw, db, lg, lb):
    b, t, c = a.shape
    per_tile = CONV_TILE // CONV_HALO
    cur = pl.BlockSpec((1, CONV_TILE, c), lambda bi, i: (bi, i, 0))
    halo = pl.BlockSpec((1, CONV_HALO, c), lambda bi, i: (bi, jnp.maximum(i * per_tile - 1, 0), 0))
    return pl.pallas_call(
        _conv_prompt_kernel,
        out_shape=jax.ShapeDtypeStruct((b, t, c), BF16),
        grid=(b, t // CONV_TILE),
        in_specs=[cur, halo, _resident(dw.shape), _resident(db.shape), _resident(lg.shape), _resident(lb.shape)],
        out_specs=cur,
        scratch_shapes=[pltpu.VMEM((CONV_HALO + CONV_TILE, c), F32)],
        compiler_params=_params(("parallel", "parallel")),
        name="conv_prompt",
    )(a, a, dw, db, lg, lb)


def _conv_sample_kernel(pad_ref, dw_ref, db_ref, lg_ref, lb_ref, o_ref):
    nb, nt = o_ref.shape[0], o_ref.shape[1]
    for bb in range(nb):
        acc = pad_ref[bb, 0:nt, :] * dw_ref[0:1, :]
        for w in range(1, CONV_W):
            acc = acc + pad_ref[bb, w:w + nt, :] * dw_ref[w:w + 1, :]
        y = _ln_silu(acc + db_ref[...], lg_ref[...], lb_ref[...])
        o_ref[bb] = y.astype(o_ref.dtype)


def _conv_sample(pad, dw, db, lg, lb, nb=8):
    bd, tp, c = pad.shape
    nt = tp - (CONV_W - 1)
    return pl.pallas_call(
        _conv_sample_kernel,
        out_shape=jax.ShapeDtypeStruct((bd, nt, c), BF16),
        grid=(bd // nb,),
        in_specs=[pl.BlockSpec((nb, tp, c), lambda i: (i, 0, 0)),
                  _resident(dw.shape), _resident(db.shape), _resident(lg.shape), _resident(lb.shape)],
        out_specs=pl.BlockSpec((nb, nt, c), lambda i: (i, 0, 0)),
        compiler_params=_params(("parallel",)),
        name="conv_sample",
    )(pad, dw, db, lg, lb)


def _mixout_kernel(h_ref, att_ref, cnv_ref, ga_ref, gc_ref, wa_ref, wc_ref, wo_ref, o_ref):
    att = _dot(att_ref[...], wa_ref[...])
    cnv = _dot(cnv_ref[...], wc_ref[...])
    mix = (ga_ref[...] * att + gc_ref[...] * cnv).astype(BF16)
    o_ref[...] = h_ref[...] + _dot(mix, wo_ref[...])


def _mixout(h, att, cnv, ga, gc, wa, wc, wo):
    n = h.shape[0]
    tile = _row_tile(n)
    row = pl.BlockSpec((tile, D_MODEL), lambda i: (i, 0))
    return pl.pallas_call(
        _mixout_kernel,
        out_shape=jax.ShapeDtypeStruct((n, D_MODEL), F32),
        grid=(n // tile,),
        in_specs=[row] * 5 + [_resident(wa.shape), _resident(wc.shape), _resident(wo.shape)],
        out_specs=row,
        compiler_params=_params(("parallel",)),
        name="mixout",
    )(h, att, cnv, ga, gc, wa, wc, wo)


def _memkv_kernel(m_ref, g_ref, w_ref, k_ref, v_ref):
    u = _rms(m_ref[...], g_ref[...]).astype(BF16)
    half = X_HEADS * X_DH
    k_ref[...] = _dot(u, w_ref[:, :half])
    v_ref[...] = _dot(u, w_ref[:, half:])


def _memkv(mem, g, w):
    n = mem.shape[0]
    tile = _row_tile(n)
    half = X_HEADS * X_DH
    out = pl.BlockSpec((tile, half), lambda i: (i, 0))
    o = jax.ShapeDtypeStruct((n, half), F32)
    return pl.pallas_call(
        _memkv_kernel,
        out_shape=(o, o),
        grid=(n // tile,),
        in_specs=[pl.BlockSpec((tile, D_MODEL), lambda i: (i, 0)), _resident(g.shape), _resident(w.shape)],
        out_specs=(out, out),
        compiler_params=_params(("parallel",)),
        name="mem_kv",
    )(mem, g, w)


def _cross_kernel(h_ref, g_ref, wq_ref, mk_ref, mv_ref, wo_ref, o_ref, *, rows_per_batch):
    h = h_ref[...]
    u = _rms(h, g_ref[...]).astype(BF16)
    q = (_dot(u, wq_ref[...]) * X_SCALE).astype(BF16)
    nb, n_mem, width = mk_ref.shape
    mk = mk_ref[...].reshape(nb * n_mem, width).astype(BF16)
    mv = mv_ref[...].reshape(nb * n_mem, width).astype(BF16)
    rows = h.shape[0]
    if nb > 1:
        rb = lax.broadcasted_iota(jnp.int32, (rows, nb * n_mem), 0) // rows_per_batch
        cb = lax.broadcasted_iota(jnp.int32, (rows, nb * n_mem), 1) // n_mem
        own = rb == cb
    outs = []
    for hd in range(X_HEADS):
        sl = slice(hd * X_DH, (hd + 1) * X_DH)
        s = _dot_nt(q[:, sl], mk[:, sl])
        if nb > 1:
            s = jnp.where(own, s, NEG)
        p = jnp.exp(s - jnp.max(s, axis=-1, keepdims=True))
        o = _dot(p.astype(BF16), mv[:, sl]) / jnp.sum(p, axis=-1, keepdims=True)
        outs.append(o.astype(BF16))
    o_ref[...] = h + _dot(jnp.concatenate(outs, axis=1), wo_ref[...])


def _cross(h, g, wq, mk, mv, wo, rows_per_batch):
    n = h.shape[0]
    tile = _row_tile(n)
    if rows_per_batch >= tile:
        nb = 1
        tiles_per_batch = rows_per_batch // tile
        mem_map = lambda i: (i // tiles_per_batch, 0, 0)
    else:
        nb = 8
        tile = nb * rows_per_batch
        mem_map = lambda i: (i, 0, 0)
    row = pl.BlockSpec((tile, D_MODEL), lambda i: (i, 0))
    mem = pl.BlockSpec((nb,) + mk.shape[1:], mem_map)
    return pl.pallas_call(
        functools.partial(_cross_kernel, rows_per_batch=rows_per_batch),
        out_shape=jax.ShapeDtypeStruct((n, D_MODEL), F32),
        grid=(n // tile,),
        in_specs=[row, _resident(g.shape), _resident(wq.shape), mem, mem, _resident(wo.shape)],
        out_specs=row,
        compiler_params=_params(("parallel",)),
        name="cross_attention",
    )(h, g, wq, mk, mv, wo)


def _t5_bucket(rel):
    n = jnp.maximum(rel, 0)
    max_exact = N_BUCKETS // 2
    log_ratio = jnp.log(jnp.maximum(n, 1).astype(F32) / max_exact) / math.log(MAX_DIST / max_exact)
    large = jnp.minimum(max_exact + (log_ratio * (N_BUCKETS - max_exact)).astype(jnp.int32), N_BUCKETS - 1)
    return jnp.where(n < max_exact, n, large)


def _bias_minus_far(table, rel, visible):
    b = jnp.moveaxis(table[_t5_bucket(rel)], -1, 0).astype(F32) - table[N_BUCKETS - 1][:, None, None]
    return jnp.where(visible[None], b, NEG)


def kernel(x_prompt, x_sample, mem_prompt, cache_k, cache_v, page_table, state_conv, cache_mem_k, cache_mem_v,
           rel_bias_table, norm_ffn1, ffn1_w_in, ffn1_w_out, norm_mix, w_in, lambda_q1, lambda_k1, lambda_q2,
           lambda_k2, subln, w_attn_o, conv_dw_w, conv_dw_b, conv_ln_g, conv_ln_b, w_conv_o, w_out, norm_cross,
           norm_mem, w_xq, w_xkv, w_xo, norm_ffn2, ffn2_w_in, ffn2_w_out, norm_final):
    depth = norm_ffn1.shape[0]
    assert depth == 1, "single-layer step"
    l = 0
    lam_init = 0.8 - 0.6 * math.exp(-0.3 * l)
    b, t, d = x_prompt.shape
    bd, tn, _ = x_sample.shape
    n_pages = page_table.shape[1]
    n_mem = mem_prompt.shape[1]
    assert d == D_MODEL and t % ATT_BLK == 0 and t % CONV_TILE == 0 and ATT_BLK > MAX_DIST
    assert n_pages % PAGES_PER_STEP == 0 and tn <= NEW_PAD and bd % 8 == 0

    row = lambda v: v.reshape(1, -1).astype(F32)
    bf = lambda w: w.astype(BF16)
    g_ffn1, g_mix, g_cross, g_mem, g_ffn2, g_fin = (row(norm_ffn1[l]), row(norm_mix[l]), row(norm_cross[l]),
                                                    row(norm_mem[l]), row(norm_ffn2[l]), row(norm_final))
    w1g, w1u, w1o = bf(ffn1_w_in[l][:, :D_FF]), bf(ffn1_w_in[l][:, D_FF:]), bf(ffn1_w_out[l])
    w2g, w2u, w2o = bf(ffn2_w_in[l][:, :D_FF]), bf(ffn2_w_in[l][:, D_FF:]), bf(ffn2_w_out[l])
    w_mix, w_ao, w_co, w_o = bf(w_in[l]), bf(w_attn_o[l]), bf(w_conv_o[l]), bf(w_out[l])
    wq, wkv, wxo = bf(w_xq[l]), bf(w_xkv[l]), bf(w_xo[l])
    lamv = jnp.stack([lambda_q1[l], lambda_k1[l], lambda_q2[l], lambda_k2[l]]).astype(F32)
    sub = row(subln[l])
    dw, db, lg, lb = conv_dw_w[l].astype(F32), row(conv_dw_b[l]), row(conv_ln_g[l]), row(conv_ln_b[l])
    table = rel_bias_table.astype(F32)

    qi = jnp.arange(ATT_BLK)[:, None]
    kj = jnp.arange(ATT_BLK)[None, :]
    bias_tiles = jnp.stack([_bias_minus_far(table, qi - kj, kj <= qi),
                            _bias_minus_far(table, qi - kj + ATT_BLK, jnp.ones((ATT_BLK, ATT_BLK), bool))], axis=1)

    xp = x_prompt.reshape(b * t, d)
    hp = _ffn(xp, g_ffn1, w1g, w1u, w1o, g_fin, False)
    qp, kp, vp, kpb, vpb, ap, gap, gcp = _mixproj(hp, g_mix, w_mix)
    att_p = _prompt_attention(qp.reshape(b, t, d), kpb.reshape(b, t, d), vpb.reshape(b, t, d), bias_tiles, lamv, sub,
                              lam_init)
    cnv_p = _conv_prompt(ap.reshape(b, t, d), dw, db, lg, lb)
    hp = _mixout(hp, att_p.reshape(b * t, d), cnv_p.reshape(b * t, d), gap, gcp, w_ao, w_co, w_o)
    mkp, mvp = _memkv(mem_prompt.reshape(b * n_mem, d), g_mem, wkv)
    hp = _cross(hp, g_cross, wq, mkp.reshape(b, n_mem, -1), mvp.reshape(b, n_mem, -1), wxo, t)
    yp = _ffn(hp, g_ffn2, w2g, w2u, w2o, g_fin, True)

    xs = x_sample.reshape(bd * tn, d)
    hs = _ffn(xs, g_ffn1, w1g, w1u, w1o, g_fin, False)
    qs, ks, vs, ksb, vsb, as_, gas, gcs = _mixproj(hs, g_mix, w_mix)
    n_hm = 2 * H_DIFF
    lane_hm = jnp.arange(d)[None, :] // D_QK
    row_hm = jnp.arange(n_hm * tn)[:, None] // tn
    q_rows = jnp.where((lane_hm == row_hm)[None], jnp.tile(qs.reshape(bd, tn, d), (1, n_hm, 1)), 0).astype(BF16)
    pad_new = lambda z: jnp.pad(z.reshape(bd, tn, d), ((0, 0), (0, NEW_PAD - tn), (0, 0)))
    q_pos = n_pages * PAGE_SIZE + jnp.arange(tn)
    last_pos = (n_pages - 1) * PAGE_SIZE + jnp.arange(PAGE_SIZE)
    expand = lambda bias: jnp.broadcast_to(bias[:, None], (H_DIFF, 2) + bias.shape[1:]).reshape(n_hm * tn, -1)
    bias_last = expand(_bias_minus_far(table, q_pos[:, None] - last_pos[None, :], jnp.ones((tn, PAGE_SIZE), bool)))
    kn = jnp.arange(NEW_PAD)
    bias_new = expand(_bias_minus_far(table, jnp.arange(tn)[:, None] - kn[None, :],
                                      kn[None, :] <= jnp.arange(tn)[:, None]))
    width = H_DIFF * HEAD_W
    att_s = _sample_attention(q_rows, cache_k[l].reshape(-1, PAGE_SIZE, width), cache_v[l].reshape(-1, PAGE_SIZE, width),
                              page_table, pad_new(ksb), pad_new(vsb), bias_last, bias_new, lamv, sub, lam_init, tn)
    pad_s = jnp.concatenate([state_conv[l].astype(F32), as_.reshape(bd, tn, d)], axis=1)
    cnv_s = _conv_sample(pad_s, dw, db, lg, lb)
    hs = _mixout(hs, att_s.reshape(bd * tn, d), cnv_s.reshape(bd * tn, d), gas, gcs, w_ao, w_co, w_o)
    hs = _cross(hs, g_cross, wq, cache_mem_k[l].reshape(bd, n_mem, -1), cache_mem_v[l].reshape(bd, n_mem, -1), wxo, tn)
    ys = _ffn(hs, g_ffn2, w2g, w2u, w2o, g_fin, True)

    return (yp.reshape(b, t, d), ys.reshape(bd, tn, d),
            kp.reshape(1, b, t, H_DIFF, HEAD_W), vp.reshape(1, b, t, H_DIFF, D_V),
            ap.reshape(b, t, d)[None, :, t - (CONV_W - 1):],
            mkp.reshape(1, b, n_mem, X_HEADS, X_DH), mvp.reshape(1, b, n_mem, X_HEADS, X_DH),
            ks.reshape(1, bd, tn, H_DIFF, HEAD_W), vs.reshape(1, bd, tn, H_DIFF, D_V),
            pad_s[None, :, tn:])
```

```python
import functools
import math

import jax
import jax.numpy as jnp
from jax import lax
from jax.experimental import pallas as pl
from jax.experimental.pallas import tpu as pltpu

F32 = jnp.float32
BF16 = jnp.bfloat16

D_MODEL = 1024
H_DIFF = 8
D_QK = 64
D_V = 128
HEAD_W = 2 * D_QK
CONV_CH = D_MODEL
CONV_W = 31
D_FF = 2816
X_HEADS = 4
X_DH = 128
N_BUCKETS = 32
MAX_DIST = 128
PAGE_SIZE = 128
EPS = 1e-6
NEG = -1e30
QK_SCALE = D_QK ** -0.5
X_SCALE = X_DH ** -0.5
LOG2E = math.log2(math.e)

SUBLANES = 8
V7X_VMEM_BYTES = 64 * 1024 * 1024
VMEM_LIMIT = 56 * 1024 * 1024

ROW_TILE = 256
ATT_BLK = 256
Q_BLKS = 2
K_BLKS = 2
CONV_TILE = 256
CONV_HALO = 32
CONV_ROWS = 32
PAGES_PER_STEP = 8
NEW_PAD = 16


def _params(sem):
    return pltpu.CompilerParams(dimension_semantics=sem, vmem_limit_bytes=VMEM_LIMIT)


def _resident(shape):
    nd = len(shape)
    return pl.BlockSpec(shape, lambda *_: (0,) * nd, pipeline_mode=pl.Buffered(1))


def _rms(x, g):
    return x * lax.rsqrt(jnp.mean(x * x, axis=-1, keepdims=True) + EPS) * g


def _dot(a, b):
    return jnp.dot(a, b, preferred_element_type=F32)


def _dot_nt(a, b):
    return lax.dot_general(a, b, (((1,), (1,)), ((), ())), preferred_element_type=F32)


def _ffn_kernel(x_ref, g_ref, wg_ref, wu_ref, wo_ref, gf_ref, o_ref, *, final_norm):
    x = x_ref[...]
    u = _rms(x, g_ref[...]).astype(BF16)
    gate = _dot(u, wg_ref[...])
    up = _dot(u, wu_ref[...])
    act = (gate * jax.nn.sigmoid(gate) * up).astype(BF16)
    h = x + 0.5 * _dot(act, wo_ref[...])
    if final_norm:
        h = _rms(h, gf_ref[...])
    o_ref[...] = h


def _row_tile(n):
    tile = min(ROW_TILE, n)
    assert n % tile == 0 and tile % 8 == 0
    return tile


def _ffn(x, g, wg, wu, wo, gf, final_norm):
    n = x.shape[0]
    tile = _row_tile(n)
    row = pl.BlockSpec((tile, D_MODEL), lambda i: (i, 0))
    return pl.pallas_call(
        functools.partial(_ffn_kernel, final_norm=final_norm),
        out_shape=jax.ShapeDtypeStruct((n, D_MODEL), F32),
        grid=(n // tile,),
        in_specs=[row, _resident(g.shape), _resident(wg.shape), _resident(wu.shape),
                  _resident(wo.shape), _resident(gf.shape)],
        out_specs=row,
        compiler_params=_params(("parallel",)),
        name="ffn",
    )(x, g, wg, wu, wo, gf)


def _mixproj_kernel(h_ref, g_ref, w_ref, wqt_ref, wvt_ref, q_ref, k_ref, v_ref, kb_ref, vb_ref, a_ref, ga_ref,
                    gc_ref, *, transposed):
    u = _rms(h_ref[...], g_ref[...]).astype(BF16)
    d = D_MODEL

    def cols(j):
        return _dot(u, w_ref[:, j * d:(j + 1) * d])

    k = cols(1)
    k_ref[...] = k
    kb_ref[...] = k.astype(BF16)
    v = cols(2)
    v_ref[...] = v
    if transposed:
        q_ref[0] = (_dot_nt(wqt_ref[...], u) * (QK_SCALE * LOG2E)).astype(BF16)
        vb_ref[0] = _dot_nt(wvt_ref[...], u).astype(BF16)
    else:
        q_ref[...] = (cols(0) * QK_SCALE).astype(BF16)
        vb_ref[...] = v.astype(BF16)
    a_ref[...] = cols(3) * jax.nn.sigmoid(cols(4))
    ga_ref[...] = jax.nn.sigmoid(cols(5))
    gc_ref[...] = jax.nn.sigmoid(cols(6))


def _mixproj(h, g, w, wqt, wvt, transposed):
    n = h.shape[0]
    tile = ATT_BLK if transposed else _row_tile(n)
    row = pl.BlockSpec((tile, D_MODEL), lambda i: (i, 0))
    f32o = jax.ShapeDtypeStruct((n, D_MODEL), F32)
    b16o = jax.ShapeDtypeStruct((n, D_MODEL), BF16)
    if transposed:
        t16o = jax.ShapeDtypeStruct((n // tile, D_MODEL, tile), BF16)
        trow = pl.BlockSpec((1, D_MODEL, tile), lambda i: (i, 0, 0))
    else:
        t16o, trow = b16o, row
    return pl.pallas_call(
        functools.partial(_mixproj_kernel, transposed=transposed),
        out_shape=(t16o, f32o, f32o, b16o, t16o, f32o, f32o, f32o),
        grid=(n // tile,),
        in_specs=[row, _resident(g.shape), _resident(w.shape), _resident(wqt.shape), _resident(wvt.shape)],
        out_specs=(trow, row, row, row, trow, row, row, row),
        compiler_params=_params(("parallel",)),
        name="mixproj",
    )(h, g, w, wqt, wvt)


def _lambda_value(lamv_ref, lam_init):
    lv = lamv_ref[...]
    d1 = jnp.sum(lv[0:1] * lv[1:2], axis=-1, keepdims=True)
    d2 = jnp.sum(lv[2:3] * lv[3:4], axis=-1, keepdims=True)
    return jnp.exp(d1) - jnp.exp(d2) + lam_init


def _head_out(o1, o2, lam, subln, lam_init):
    o = o1 - lam * o2
    return o * lax.rsqrt(jnp.mean(o * o, axis=-1, keepdims=True) + EPS) * subln * (1.0 - lam_init)


def _softmax_step(carry, s, v):
    m, l, acc = carry
    m_new = jnp.maximum(m, jnp.max(s, axis=-1, keepdims=True))
    alpha = jnp.exp(m - m_new)
    p = jnp.exp(s - m_new)
    l = alpha * l + jnp.sum(p, axis=-1, keepdims=True)
    acc = alpha * acc + _dot(p.astype(BF16), v)
    return m_new, l, acc


def _pattn_kernel(qt_ref, k_ref, vt_ref, bias_ref, lamv_ref, subln_ref, o_ref, s_ref, p_ref, acc_ref, *,
                  lam_init):
    i = pl.program_id(2)
    blk = ATT_BLK
    nq, nk = Q_BLKS * blk, K_BLKS * blk
    first = (nq // nk) * i
    qt = jnp.concatenate([qt_ref[0, c] for c in range(Q_BLKS)], axis=1)
    row = lax.broadcasted_iota(jnp.int32, qt.shape, 0)
    zero = jnp.zeros_like(qt)
    qqt = jnp.concatenate([jnp.where(row < D_QK, qt, zero), jnp.where(row >= D_QK, qt, zero)], axis=1)

    def scores(j, n=nk):
        return _dot(k_ref[0, pl.ds(pl.multiple_of(j * nk, nk), n), :], qqt)

    def values(j, p):
        out = _dot(vt_ref[0, K_BLKS * j], p[:blk])
        for c in range(1, p.shape[0] // blk):
            out = out + _dot(vt_ref[0, K_BLKS * j + c], p[c * blk:(c + 1) * blk])
        return out

    def softmax(s, m, l):
        m_new = jnp.maximum(m, jnp.max(s, axis=0, keepdims=True))
        alpha = jnp.exp2(m - m_new)
        p = jnp.exp2(s - m_new)
        return m_new, alpha * l + jnp.sum(p, axis=0, keepdims=True), alpha, p.astype(BF16)

    n_far = jnp.maximum(first - 1, 0)
    s_ref[...] = scores(0)
    p_ref[...] = jnp.zeros(p_ref.shape, p_ref.dtype)
    acc_ref[...] = jnp.zeros(acc_ref.shape, acc_ref.dtype)

    def far_step(j, carry):
        m, l, alpha_prev = carry
        acc_ref[...] = alpha_prev * acc_ref[...] + values(jnp.maximum(j - 1, 0), p_ref[...])
        m, l, alpha, p = softmax(s_ref[...], m, l)
        p_ref[...] = p
        s_ref[...] = scores(j + 1)
        return m, l, alpha

    init = (jnp.full((1, 2 * nq), NEG, F32), jnp.zeros((1, 2 * nq), F32), jnp.ones((1, 2 * nq), F32))
    m, l, alpha_prev = lax.fori_loop(0, n_far, far_step, init)
    acc = alpha_prev * acc_ref[...] + values(jnp.maximum(n_far - 1, 0), p_ref[...])

    j_near = jnp.maximum(first - 1, 0)
    no_near = jnp.where(i == 0, 2.0 * NEG, 0.0).astype(F32)
    m, l, alpha, p = softmax(s_ref[...] + (bias_ref[0, 0:nk, :] + no_near), m, l)
    acc = alpha * acc + values(j_near, p)
    m, l, alpha, p = softmax(scores(first, nq) + bias_ref[0, nk:, :], m, l)
    acc = alpha * acc + values(first, p)

    o = acc / l
    o = o[:, :nq] - _lambda_value(lamv_ref, lam_init) * o[:, nq:]
    o = o * lax.rsqrt(jnp.mean(o * o, axis=0, keepdims=True) + EPS) * subln_ref[...] * (1.0 - lam_init)
    o_ref[0] = o.T.astype(o_ref.dtype)


def _prompt_attention(qt, kb, vt, bias_tiles, lamv, subln_col, lam_init):
    b, t, _ = kb.shape
    blk = ATT_BLK
    nblk = t // blk
    nq, nk = Q_BLKS * blk, K_BLKS * blk
    assert nq % nk == 0 and t % nq == 0
    return pl.pallas_call(
        functools.partial(_pattn_kernel, lam_init=lam_init),
        out_shape=jax.ShapeDtypeStruct((b, t, H_DIFF * D_V), BF16),
        grid=(b, H_DIFF, nblk // Q_BLKS),
        in_specs=[pl.BlockSpec((1, Q_BLKS, HEAD_W, blk), lambda bi, h, i: (bi, i, h, 0)),
                  pl.BlockSpec((1, t, HEAD_W), lambda bi, h, i: (bi, 0, h)),
                  pl.BlockSpec((1, nblk, D_V, blk), lambda bi, h, i: (bi, 0, h, 0)),
                  pl.BlockSpec((1, nk + nq, 2 * nq), lambda bi, h, i: (h, 0, 0)),
                  pl.BlockSpec(lamv.shape, lambda bi, h, i: (0, 0)),
                  pl.BlockSpec(subln_col.shape, lambda bi, h, i: (0, 0))],
        out_specs=pl.BlockSpec((1, nq, HEAD_W), lambda bi, h, i: (bi, i, h)),
        scratch_shapes=[pltpu.VMEM((nk, 2 * nq), F32), pltpu.VMEM((nk, 2 * nq), BF16),
                        pltpu.VMEM((D_V, 2 * nq), F32)],
        compiler_params=_params(("parallel", "parallel", "arbitrary")),
        name="prompt_attention",
    )(qt, kb, vt, bias_tiles, lamv, subln_col)


def _sattn_kernel(pt_ref, q_ref, *refs, lam_init, n_steps):
    del pt_ref
    npg = PAGES_PER_STEP
    k_refs, v_refs = refs[:npg], refs[npg:2 * npg]
    (kn_ref, vn_ref, blast_ref, bnew_ref, lamv_ref, subln_ref, o_ref, m_ref, l_ref, acc_ref) = refs[2 * npg:]
    c = pl.program_id(1)

    @pl.when(c == 0)
    def _():
        m_ref[...] = jnp.full(m_ref.shape, NEG, F32)
        l_ref[...] = jnp.zeros(l_ref.shape, F32)
        acc_ref[...] = jnp.zeros(acc_ref.shape, F32)

    q = q_ref[0]
    s_pages = [_dot_nt(q, k_refs[p][0].astype(BF16)) for p in range(npg)]
    is_last = (c == n_steps - 1).astype(F32)
    s_pages[-1] = s_pages[-1] + is_last * blast_ref[...]
    s = jnp.concatenate(s_pages, axis=1)
    m = m_ref[...]
    m_new = jnp.maximum(m, jnp.max(s, axis=-1, keepdims=True))
    alpha = jnp.exp(m - m_new)
    p = jnp.exp(s - m_new)
    l_ref[...] = alpha * l_ref[...] + jnp.sum(p, axis=-1, keepdims=True)
    pb = p.astype(BF16)
    pv = _dot(pb[:, :PAGE_SIZE], v_refs[0][0].astype(BF16))
    for pg in range(1, npg):
        pv = pv + _dot(pb[:, pg * PAGE_SIZE:(pg + 1) * PAGE_SIZE], v_refs[pg][0].astype(BF16))
    acc_ref[...] = alpha * acc_ref[...] + pv
    m_ref[...] = m_new

    @pl.when(c == n_steps - 1)
    def _():
        s_new = _dot_nt(q, kn_ref[0]) + bnew_ref[...]
        m2, l2, acc2 = _softmax_step((m_ref[...], l_ref[...], acc_ref[...]), s_new, vn_ref[0])
        lam = _lambda_value(lamv_ref, lam_init)
        rows = acc2.shape[0] // H_DIFF
        nq = rows // 2
        for h in range(H_DIFF):
            o = acc2[h * rows:(h + 1) * rows, h * D_V:(h + 1) * D_V] / l2[h * rows:(h + 1) * rows]
            o_ref[0, :, h * D_V:(h + 1) * D_V] = _head_out(o[:nq], o[nq:], lam, subln_ref[...], lam_init
                                                           ).astype(o_ref.dtype)


def _sample_attention(q_rows, cache_k, cache_v, page_table, k_new, v_new, bias_last, bias_new, lamv, subln,
                      lam_init, nq):
    bd, n_pages = page_table.shape
    npg = PAGES_PER_STEP
    n_steps = n_pages // npg
    rows = q_rows.shape[1]
    width = H_DIFF * HEAD_W

    def page_spec(p):
        return pl.BlockSpec((1, PAGE_SIZE, width), lambda b, c, pt: (pt[b, c * npg + p], 0, 0))

    per_batch = lambda shape: pl.BlockSpec((1,) + shape, lambda b, c, pt: (b, 0, 0))
    const = lambda shape: pl.BlockSpec(shape, lambda b, c, pt: (0, 0))
    grid_spec = pltpu.PrefetchScalarGridSpec(
        num_scalar_prefetch=1,
        grid=(bd, n_steps),
        in_specs=[per_batch((rows, width))]
        + [page_spec(p) for p in range(npg)] * 2
        + [per_batch((NEW_PAD, width)), per_batch((NEW_PAD, width)),
           const(bias_last.shape), const(bias_new.shape), const(lamv.shape), const(subln.shape)],
        out_specs=per_batch((nq, width)),
        scratch_shapes=[pltpu.VMEM((rows, 1), F32), pltpu.VMEM((rows, 1), F32), pltpu.VMEM((rows, width), F32)],
    )
    return pl.pallas_call(
        functools.partial(_sattn_kernel, lam_init=lam_init, n_steps=n_steps),
        out_shape=jax.ShapeDtypeStruct((bd, nq, width), BF16),
        grid_spec=grid_spec,
        compiler_params=_params(("parallel", "arbitrary")),
        name="sample_attention",
    )(page_table, q_rows, *([cache_k] * npg), *([cache_v] * npg), k_new, v_new, bias_last, bias_new, lamv, subln)


def _ln_silu(y, g, b):
    mu = jnp.mean(y, axis=-1, keepdims=True)
    yc = y - mu
    var = jnp.mean(yc * yc, axis=-1, keepdims=True)
    z = yc * lax.rsqrt(var + EPS) * g + b
    return z * jax.nn.sigmoid(z)


def _conv_prompt_kernel(cur_ref, halo_ref, dw_ref, db_ref, lg_ref, lb_ref, o_ref, win_ref):
    i = pl.program_id(1)
    keep = (i > 0).astype(F32)
    win_ref[0, 0:CONV_HALO, :] = halo_ref[0] * keep
    win_ref[0, CONV_HALO:, :] = cur_ref[0]
    first = CONV_HALO - (CONV_W - 1)
    span = CONV_TILE + max((first + w) // SUBLANES * SUBLANES for w in range(CONV_W) if (first + w) % SUBLANES)
    assert SUBLANES - 1 + span <= CONV_HALO + CONV_TILE
    for s in range(1, SUBLANES):
        win_ref[s, 0:span, :] = win_ref[0, s:s + span, :]

    for r0 in range(0, CONV_TILE, CONV_ROWS):
        acc = None
        for w in range(CONV_W):
            base, s = (first + w) // SUBLANES * SUBLANES, (first + w) % SUBLANES
            term = win_ref[s, r0 + base:r0 + base + CONV_ROWS, :] * dw_ref[w:w + 1, :]
            acc = term if acc is None else acc + term
        y = _ln_silu(acc + db_ref[...], lg_ref[...], lb_ref[...])
        o_ref[0, r0:r0 + CONV_ROWS, :] = y.astype(o_ref.dtype)


def _conv_prompt(a, dw, db, lg, lb):
    b, t, c = a.shape
    per_tile = CONV_TILE // CONV_HALO
    cur = pl.BlockSpec((1, CONV_TILE, c), lambda bi, i: (bi, i, 0))
    halo = pl.BlockSpec((1, CONV_HALO, c), lambda bi, i: (bi, jnp.maximum(i * per_tile - 1, 0), 0))
    return pl.pallas_call(
        _conv_prompt_kernel,
        out_shape=jax.ShapeDtypeStruct((b, t, c), BF16),
        grid=(b, t // CONV_TILE),
        in_specs=[cur, halo, _resident(dw.shape), _resident(db.shape), _resident(lg.shape), _resident(lb.shape)],
        out_specs=cur,
        scratch_shapes=[pltpu.VMEM((SUBLANES, CONV_HALO + CONV_TILE, c), F32)],
        compiler_params=_params(("parallel", "parallel")),
        name="conv_prompt",
    )(a, a, dw, db, lg, lb)


def _conv_sample_kernel(pad_ref, dw_ref, db_ref, lg_ref, lb_ref, o_ref):
    nb, nt = o_ref.shape[0], o_ref.shape[1]
    for bb in range(nb):
        acc = pad_ref[bb, 0:nt, :] * dw_ref[0:1, :]
        for w in range(1, CONV_W):
            acc = acc + pad_ref[bb, w:w + nt, :] * dw_ref[w:w + 1, :]
        y = _ln_silu(acc + db_ref[...], lg_ref[...], lb_ref[...])
        o_ref[bb] = y.astype(o_ref.dtype)


def _conv_sample(pad, dw, db, lg, lb, nb=8):
    bd, tp, c = pad.shape
    nt = tp - (CONV_W - 1)
    return pl.pallas_call(
        _conv_sample_kernel,
        out_shape=jax.ShapeDtypeStruct((bd, nt, c), BF16),
        grid=(bd // nb,),
        in_specs=[pl.BlockSpec((nb, tp, c), lambda i: (i, 0, 0)),
                  _resident(dw.shape), _resident(db.shape), _resident(lg.shape), _resident(lb.shape)],
        out_specs=pl.BlockSpec((nb, nt, c), lambda i: (i, 0, 0)),
        compiler_params=_params(("parallel",)),
        name="conv_sample",
    )(pad, dw, db, lg, lb)


def _mixout_kernel(h_ref, att_ref, cnv_ref, ga_ref, gc_ref, wa_ref, wc_ref, wo_ref, o_ref):
    att = _dot(att_ref[...], wa_ref[...])
    cnv = _dot(cnv_ref[...], wc_ref[...])
    mix = (ga_ref[...] * att + gc_ref[...] * cnv).astype(BF16)
    o_ref[...] = h_ref[...] + _dot(mix, wo_ref[...])


def _mixout(h, att, cnv, ga, gc, wa, wc, wo):
    n = h.shape[0]
    tile = _row_tile(n)
    row = pl.BlockSpec((tile, D_MODEL), lambda i: (i, 0))
    return pl.pallas_call(
        _mixout_kernel,
        out_shape=jax.ShapeDtypeStruct((n, D_MODEL), F32),
        grid=(n // tile,),
        in_specs=[row] * 5 + [_resident(wa.shape), _resident(wc.shape), _resident(wo.shape)],
        out_specs=row,
        compiler_params=_params(("parallel",)),
        name="mixout",
    )(h, att, cnv, ga, gc, wa, wc, wo)


def _memkv_kernel(m_ref, g_ref, w_ref, k_ref, v_ref):
    u = _rms(m_ref[...], g_ref[...]).astype(BF16)
    half = X_HEADS * X_DH
    k_ref[...] = _dot(u, w_ref[:, :half])
    v_ref[...] = _dot(u, w_ref[:, half:])


def _memkv(mem, g, w):
    n = mem.shape[0]
    tile = _row_tile(n)
    half = X_HEADS * X_DH
    out = pl.BlockSpec((tile, half), lambda i: (i, 0))
    o = jax.ShapeDtypeStruct((n, half), F32)
    return pl.pallas_call(
        _memkv_kernel,
        out_shape=(o, o),
        grid=(n // tile,),
        in_specs=[pl.BlockSpec((tile, D_MODEL), lambda i: (i, 0)), _resident(g.shape), _resident(w.shape)],
        out_specs=(out, out),
        compiler_params=_params(("parallel",)),
        name="mem_kv",
    )(mem, g, w)


def _cross_kernel(h_ref, g_ref, wq_ref, mk_ref, mv_ref, wo_ref, o_ref, *, rows_per_batch):
    h = h_ref[...]
    u = _rms(h, g_ref[...]).astype(BF16)
    q = (_dot(u, wq_ref[...]) * X_SCALE).astype(BF16)
    nb, n_mem, width = mk_ref.shape
    mk = mk_ref[...].reshape(nb * n_mem, width).astype(BF16)
    mv = mv_ref[...].reshape(nb * n_mem, width).astype(BF16)
    rows = h.shape[0]
    if nb > 1:
        rb = lax.broadcasted_iota(jnp.int32, (rows, nb * n_mem), 0) // rows_per_batch
        cb = lax.broadcasted_iota(jnp.int32, (rows, nb * n_mem), 1) // n_mem
        own = rb == cb
    outs = []
    for hd in range(X_HEADS):
        sl = slice(hd * X_DH, (hd + 1) * X_DH)
        s = _dot_nt(q[:, sl], mk[:, sl])
        if nb > 1:
            s = jnp.where(own, s, NEG)
        p = jnp.exp(s - jnp.max(s, axis=-1, keepdims=True))
        o = _dot(p.astype(BF16), mv[:, sl]) / jnp.sum(p, axis=-1, keepdims=True)
        outs.append(o.astype(BF16))
    o_ref[...] = h + _dot(jnp.concatenate(outs, axis=1), wo_ref[...])


def _cross(h, g, wq, mk, mv, wo, rows_per_batch):
    n = h.shape[0]
    tile = _row_tile(n)
    if rows_per_batch >= tile:
        nb = 1
        tiles_per_batch = rows_per_batch // tile
        mem_map = lambda i: (i // tiles_per_batch, 0, 0)
    else:
        nb = 8
        tile = nb * rows_per_batch
        mem_map = lambda i: (i, 0, 0)
    row = pl.BlockSpec((tile, D_MODEL), lambda i: (i, 0))
    mem = pl.BlockSpec((nb,) + mk.shape[1:], mem_map)
    return pl.pallas_call(
        functools.partial(_cross_kernel, rows_per_batch=rows_per_batch),
        out_shape=jax.ShapeDtypeStruct((n, D_MODEL), F32),
        grid=(n // tile,),
        in_specs=[row, _resident(g.shape), _resident(wq.shape), mem, mem, _resident(wo.shape)],
        out_specs=row,
        compiler_params=_params(("parallel",)),
        name="cross_attention",
    )(h, g, wq, mk, mv, wo)


def _t5_bucket(rel):
    n = jnp.maximum(rel, 0)
    max_exact = N_BUCKETS // 2
    log_ratio = jnp.log(jnp.maximum(n, 1).astype(F32) / max_exact) / math.log(MAX_DIST / max_exact)
    large = jnp.minimum(max_exact + (log_ratio * (N_BUCKETS - max_exact)).astype(jnp.int32), N_BUCKETS - 1)
    return jnp.where(n < max_exact, n, large)


def _bias_minus_far(table, rel, visible):
    b = jnp.moveaxis(table[_t5_bucket(rel)], -1, 0).astype(F32) - table[N_BUCKETS - 1][:, None, None]
    return jnp.where(visible[None], b, NEG)


def _toeplitz_kq(g, n_keys, n_queries):
    h, period = g.shape
    assert n_queries <= period - 1
    return jnp.tile(g, (1, n_keys))[:, :n_keys * (period - 1)].reshape(h, n_keys, period - 1)[:, :, :n_queries]


def _prompt_bias_tiles(table):
    nq, nk = Q_BLKS * ATT_BLK, K_BLKS * ATT_BLK
    n_keys = nk + nq
    period = n_keys + nq
    diff = (jnp.arange(period) + n_keys) % period - n_keys
    rel = diff + nk
    by_rel = _bias_minus_far(table, jnp.maximum(rel, 0)[None, :], (rel >= 0)[None, :])[:, 0]
    tile = _toeplitz_kq(by_rel * LOG2E, n_keys, nq)
    return jnp.tile(tile, (1, 1, 2))


def kernel(x_prompt, x_sample, mem_prompt, cache_k, cache_v, page_table, state_conv, cache_mem_k, cache_mem_v,
           rel_bias_table, norm_ffn1, ffn1_w_in, ffn1_w_out, norm_mix, w_in, lambda_q1, lambda_k1, lambda_q2,
           lambda_k2, subln, w_attn_o, conv_dw_w, conv_dw_b, conv_ln_g, conv_ln_b, w_conv_o, w_out, norm_cross,
           norm_mem, w_xq, w_xkv, w_xo, norm_ffn2, ffn2_w_in, ffn2_w_out, norm_final):
    depth = norm_ffn1.shape[0]
    assert depth == 1, "single-layer step"
    l = 0
    lam_init = 0.8 - 0.6 * math.exp(-0.3 * l)
    b, t, d = x_prompt.shape
    bd, tn, _ = x_sample.shape
    n_pages = page_table.shape[1]
    n_mem = mem_prompt.shape[1]
    assert d == D_MODEL and t % (Q_BLKS * ATT_BLK) == 0 and t % CONV_TILE == 0 and ATT_BLK > MAX_DIST
    assert n_pages % PAGES_PER_STEP == 0 and tn <= NEW_PAD and bd % 8 == 0

    row = lambda v: v.reshape(1, -1).astype(F32)
    bf = lambda w: w.astype(BF16)
    g_ffn1, g_mix, g_cross, g_mem, g_ffn2, g_fin = (row(norm_ffn1[l]), row(norm_mix[l]), row(norm_cross[l]),
                                                    row(norm_mem[l]), row(norm_ffn2[l]), row(norm_final))
    w1g, w1u, w1o = bf(ffn1_w_in[l][:, :D_FF]), bf(ffn1_w_in[l][:, D_FF:]), bf(ffn1_w_out[l])
    w2g, w2u, w2o = bf(ffn2_w_in[l][:, :D_FF]), bf(ffn2_w_in[l][:, D_FF:]), bf(ffn2_w_out[l])
    w_mix, w_ao, w_co, w_o = bf(w_in[l]), bf(w_attn_o[l]), bf(w_conv_o[l]), bf(w_out[l])
    wqt, wvt = w_mix[:, :d].T, w_mix[:, 2 * d:3 * d].T
    wq, wkv, wxo = bf(w_xq[l]), bf(w_xkv[l]), bf(w_xo[l])
    lamv = jnp.stack([lambda_q1[l], lambda_k1[l], lambda_q2[l], lambda_k2[l]]).astype(F32)
    sub = row(subln[l])
    dw, db, lg, lb = conv_dw_w[l].astype(F32), row(conv_dw_b[l]), row(conv_ln_g[l]), row(conv_ln_b[l])
    table = rel_bias_table.astype(F32)

    blk = ATT_BLK
    bias_tiles = _prompt_bias_tiles(table)

    xp = x_prompt.reshape(b * t, d)
    hp = _ffn(xp, g_ffn1, w1g, w1u, w1o, g_fin, False)
    qpt, kp, vp, kpb, vpt, ap, gap, gcp = _mixproj(hp, g_mix, w_mix, wqt, wvt, True)
    att_p = _prompt_attention(qpt.reshape(b, t // blk, d, blk), kpb.reshape(b, t, d), vpt.reshape(b, t // blk, d, blk),
                              bias_tiles, lamv, sub.reshape(-1, 1), lam_init)
    cnv_p = _conv_prompt(ap.reshape(b, t, d), dw, db, lg, lb)
    hp = _mixout(hp, att_p.reshape(b * t, d), cnv_p.reshape(b * t, d), gap, gcp, w_ao, w_co, w_o)
    mkp, mvp = _memkv(mem_prompt.reshape(b * n_mem, d), g_mem, wkv)
    hp = _cross(hp, g_cross, wq, mkp.reshape(b, n_mem, -1), mvp.reshape(b, n_mem, -1), wxo, t)
    yp = _ffn(hp, g_ffn2, w2g, w2u, w2o, g_fin, True)

    xs = x_sample.reshape(bd * tn, d)
    hs = _ffn(xs, g_ffn1, w1g, w1u, w1o, g_fin, False)
    qs, ks, vs, ksb, vsb, as_, gas, gcs = _mixproj(hs, g_mix, w_mix, wqt, wvt, False)
    n_hm = 2 * H_DIFF
    lane_hm = jnp.arange(d)[None, :] // D_QK
    row_hm = jnp.arange(n_hm * tn)[:, None] // tn
    q_rows = jnp.where((lane_hm == row_hm)[None], jnp.tile(qs.reshape(bd, tn, d), (1, n_hm, 1)), 0).astype(BF16)
    pad_new = lambda z: jnp.pad(z.reshape(bd, tn, d), ((0, 0), (0, NEW_PAD - tn), (0, 0)))
    q_pos = n_pages * PAGE_SIZE + jnp.arange(tn)
    last_pos = (n_pages - 1) * PAGE_SIZE + jnp.arange(PAGE_SIZE)
    expand = lambda bias: jnp.broadcast_to(bias[:, None], (H_DIFF, 2) + bias.shape[1:]).reshape(n_hm * tn, -1)
    bias_last = expand(_bias_minus_far(table, q_pos[:, None] - last_pos[None, :], jnp.ones((tn, PAGE_SIZE), bool)))
    kn = jnp.arange(NEW_PAD)
    bias_new = expand(_bias_minus_far(table, jnp.arange(tn)[:, None] - kn[None, :],
                                      kn[None, :] <= jnp.arange(tn)[:, None]))
    width = H_DIFF * HEAD_W
    att_s = _sample_attention(q_rows, cache_k.reshape(-1, PAGE_SIZE, width), cache_v.reshape(-1, PAGE_SIZE, width),
                              page_table, pad_new(ksb), pad_new(vsb), bias_last, bias_new, lamv, sub, lam_init, tn)
    pad_s = jnp.concatenate([state_conv[l].astype(F32), as_.reshape(bd, tn, d)], axis=1)
    cnv_s = _conv_sample(pad_s, dw, db, lg, lb)
    hs = _mixout(hs, att_s.reshape(bd * tn, d), cnv_s.reshape(bd * tn, d), gas, gcs, w_ao, w_co, w_o)
    hs = _cross(hs, g_cross, wq, cache_mem_k.reshape(bd, n_mem, -1), cache_mem_v.reshape(bd, n_mem, -1), wxo, tn)
    ys = _ffn(hs, g_ffn2, w2g, w2u, w2o, g_fin, True)

    return (yp.reshape(b, t, d), ys.reshape(bd, tn, d),
            kp.reshape(1, b, t, H_DIFF, HEAD_W), vp.reshape(1, b, t, H_DIFF, D_V),
            ap.reshape(b, t, d)[None, :, t - (CONV_W - 1):],
            mkp.reshape(1, b, n_mem, X_HEADS, X_DH), mvp.reshape(1, b, n_mem, X_HEADS, X_DH),
            ks.reshape(1, bd, tn, H_DIFF, HEAD_W), vs.reshape(1, bd, tn, H_DIFF, D_V),
            pad_s[None, :, tn:])
```

```python
import functools
import math

import jax
import jax.numpy as jnp
from jax import lax
from jax.experimental import pallas as pl
from jax.experimental.pallas import tpu as pltpu

F32 = jnp.float32
BF16 = jnp.bfloat16

D_MODEL = 1024
H_DIFF = 8
D_QK = 64
D_V = 128
HEAD_W = 2 * D_QK
CONV_CH = D_MODEL
CONV_W = 31
D_FF = 2816
X_HEADS = 4
X_DH = 128
N_BUCKETS = 32
MAX_DIST = 128
PAGE_SIZE = 128
EPS = 1e-6
NEG = -1e30
QK_SCALE = D_QK ** -0.5
X_SCALE = X_DH ** -0.5
LOG2E = math.log2(math.e)

SUBLANES = 8
V7X_VMEM_BYTES = 64 * 1024 * 1024
VMEM_LIMIT = 56 * 1024 * 1024

ROW_TILE = 256
ATT_BLK = 256
Q_BLKS = 2
K_BLKS = 2
CONV_TILE = 256
CONV_HALO = 32
CONV_ROWS = 32
PAGES_PER_STEP = 8


def _params(sem):
    return pltpu.CompilerParams(dimension_semantics=sem, vmem_limit_bytes=VMEM_LIMIT)


def _resident(shape):
    nd = len(shape)
    return pl.BlockSpec(shape, lambda *_: (0,) * nd, pipeline_mode=pl.Buffered(1))


def _rms(x, g):
    return x * lax.rsqrt(jnp.mean(x * x, axis=-1, keepdims=True) + EPS) * g


def _dot(a, b):
    return jnp.dot(a, b, preferred_element_type=F32)


def _dot_nt(a, b):
    return lax.dot_general(a, b, (((1,), (1,)), ((), ())), preferred_element_type=F32)


def _ffn_kernel(x_ref, g_ref, wg_ref, wu_ref, wo_ref, gf_ref, o_ref, *, final_norm):
    x = x_ref[...]
    u = _rms(x, g_ref[...]).astype(BF16)
    gate = _dot(u, wg_ref[...])
    up = _dot(u, wu_ref[...])
    act = (gate * jax.nn.sigmoid(gate) * up).astype(BF16)
    h = x + 0.5 * _dot(act, wo_ref[...])
    if final_norm:
        h = _rms(h, gf_ref[...])
    o_ref[...] = h


def _row_tile(n):
    tile = min(ROW_TILE, n)
    assert n % tile == 0 and tile % 8 == 0
    return tile


def _ffn(x, g, wg, wu, wo, gf, final_norm):
    n = x.shape[0]
    tile = _row_tile(n)
    row = pl.BlockSpec((tile, D_MODEL), lambda i: (i, 0))
    return pl.pallas_call(
        functools.partial(_ffn_kernel, final_norm=final_norm),
        out_shape=jax.ShapeDtypeStruct((n, D_MODEL), F32),
        grid=(n // tile,),
        in_specs=[row, _resident(g.shape), _resident(wg.shape), _resident(wu.shape),
                  _resident(wo.shape), _resident(gf.shape)],
        out_specs=row,
        compiler_params=_params(("parallel",)),
        name="ffn",
    )(x, g, wg, wu, wo, gf)


def _mixproj_kernel(h_ref, g_ref, w_ref, wqt_ref, wvt_ref, q_ref, k_ref, v_ref, kb_ref, vb_ref, a_ref, ga_ref,
                    gc_ref, *, transposed):
    u = _rms(h_ref[...], g_ref[...]).astype(BF16)
    d = D_MODEL

    def cols(j):
        return _dot(u, w_ref[:, j * d:(j + 1) * d])

    k = cols(1)
    k_ref[...] = k
    kb_ref[...] = k.astype(BF16)
    v = cols(2)
    v_ref[...] = v
    if transposed:
        q_ref[0] = (_dot_nt(wqt_ref[...], u) * (QK_SCALE * LOG2E)).astype(BF16)
        vb_ref[0] = _dot_nt(wvt_ref[...], u).astype(BF16)
    else:
        q_ref[...] = (cols(0) * (QK_SCALE * LOG2E)).astype(BF16)
        vb_ref[...] = v.astype(BF16)
    a_ref[...] = cols(3) * jax.nn.sigmoid(cols(4))
    ga_ref[...] = jax.nn.sigmoid(cols(5))
    gc_ref[...] = jax.nn.sigmoid(cols(6))


def _mixproj(h, g, w, wqt, wvt, transposed):
    n = h.shape[0]
    tile = ATT_BLK if transposed else _row_tile(n)
    row = pl.BlockSpec((tile, D_MODEL), lambda i: (i, 0))
    f32o = jax.ShapeDtypeStruct((n, D_MODEL), F32)
    b16o = jax.ShapeDtypeStruct((n, D_MODEL), BF16)
    if transposed:
        t16o = jax.ShapeDtypeStruct((n // tile, D_MODEL, tile), BF16)
        trow = pl.BlockSpec((1, D_MODEL, tile), lambda i: (i, 0, 0))
    else:
        t16o, trow = b16o, row
    return pl.pallas_call(
        functools.partial(_mixproj_kernel, transposed=transposed),
        out_shape=(t16o, f32o, f32o, b16o, t16o, f32o, f32o, f32o),
        grid=(n // tile,),
        in_specs=[row, _resident(g.shape), _resident(w.shape), _resident(wqt.shape), _resident(wvt.shape)],
        out_specs=(trow, row, row, row, trow, row, row, row),
        compiler_params=_params(("parallel",)),
        name="mixproj",
    )(h, g, w, wqt, wvt)


def _lambda_value(lamv_ref, lam_init):
    lv = lamv_ref[...]
    d1 = jnp.sum(lv[0:1] * lv[1:2], axis=-1, keepdims=True)
    d2 = jnp.sum(lv[2:3] * lv[3:4], axis=-1, keepdims=True)
    return jnp.exp(d1) - jnp.exp(d2) + lam_init


def _head_out(o1, o2, lam, subln, lam_init):
    o = o1 - lam * o2
    return o * lax.rsqrt(jnp.mean(o * o, axis=-1, keepdims=True) + EPS) * subln * (1.0 - lam_init)


def _pattn_kernel(qt_ref, k_ref, vt_ref, bias_ref, lamv_ref, subln_ref, o_ref, s_ref, p_ref, acc_ref, *,
                  lam_init):
    i = pl.program_id(2)
    blk = ATT_BLK
    nq, nk = Q_BLKS * blk, K_BLKS * blk
    first = (nq // nk) * i
    qt = jnp.concatenate([qt_ref[0, c] for c in range(Q_BLKS)], axis=1)
    row = lax.broadcasted_iota(jnp.int32, qt.shape, 0)
    zero = jnp.zeros_like(qt)
    qqt = jnp.concatenate([jnp.where(row < D_QK, qt, zero), jnp.where(row >= D_QK, qt, zero)], axis=1)

    def scores(j, n=nk):
        return _dot(k_ref[0, pl.ds(pl.multiple_of(j * nk, nk), n), :], qqt)

    def values(j, p):
        out = _dot(vt_ref[0, K_BLKS * j], p[:blk])
        for c in range(1, p.shape[0] // blk):
            out = out + _dot(vt_ref[0, K_BLKS * j + c], p[c * blk:(c + 1) * blk])
        return out

    def softmax(s, m, l):
        m_new = jnp.maximum(m, jnp.max(s, axis=0, keepdims=True))
        alpha = jnp.exp2(m - m_new)
        p = jnp.exp2(s - m_new)
        return m_new, alpha * l + jnp.sum(p, axis=0, keepdims=True), alpha, p.astype(BF16)

    def absent(flag):
        return jnp.where(flag, 2.0 * NEG, 0.0).astype(F32)

    n_far = jnp.maximum(first - 1, 0)
    s_ref[...] = scores(0)
    init = (jnp.full((1, 2 * nq), NEG, F32), jnp.zeros((1, 2 * nq), F32))
    m, l, _, p = softmax(scores(first, nq) + bias_ref[0, nk:, :], *init)
    acc = values(first, p)
    j_near = jnp.maximum(first - 1, 0)
    m, l, alpha, p = softmax(scores(j_near) + (bias_ref[0, 0:nk, :] + absent(first == 0)), m, l)
    acc_ref[...] = alpha * acc + values(j_near, p)
    p_ref[...] = jnp.zeros(p_ref.shape, p_ref.dtype)

    def far_step(j, carry):
        m, l, alpha_prev = carry
        acc_ref[...] = alpha_prev * acc_ref[...] + values(jnp.maximum(j - 1, 0), p_ref[...])
        m, l, alpha, p = softmax(s_ref[...], m, l)
        p_ref[...] = p
        s_ref[...] = scores(j + 1)
        return m, l, alpha

    n_loop = jnp.maximum(n_far - 1, 0)
    m, l, alpha_prev = lax.fori_loop(0, n_loop, far_step, (m, l, jnp.ones((1, 2 * nq), F32)))
    acc = alpha_prev * acc_ref[...] + values(jnp.maximum(n_loop - 1, 0), p_ref[...])
    m, l, alpha, p = softmax(s_ref[...] + absent(n_far == 0), m, l)
    acc = alpha * acc + values(n_loop, p)

    o = acc / l
    o = o[:, :nq] - _lambda_value(lamv_ref, lam_init) * o[:, nq:]
    o = o * lax.rsqrt(jnp.mean(o * o, axis=0, keepdims=True) + EPS) * subln_ref[...] * (1.0 - lam_init)
    o_ref[0] = o.T.astype(o_ref.dtype)


def _prompt_attention(qt, kb, vt, bias_tiles, lamv, subln_col, lam_init):
    b, t, _ = kb.shape
    blk = ATT_BLK
    nblk = t // blk
    nq, nk = Q_BLKS * blk, K_BLKS * blk
    assert nq % nk == 0 and t % nq == 0
    return pl.pallas_call(
        functools.partial(_pattn_kernel, lam_init=lam_init),
        out_shape=jax.ShapeDtypeStruct((b, t, H_DIFF * D_V), BF16),
        grid=(b, H_DIFF, nblk // Q_BLKS),
        in_specs=[pl.BlockSpec((1, Q_BLKS, HEAD_W, blk), lambda bi, h, i: (bi, i, h, 0)),
                  pl.BlockSpec((1, t, HEAD_W), lambda bi, h, i: (bi, 0, h)),
                  pl.BlockSpec((1, nblk, D_V, blk), lambda bi, h, i: (bi, 0, h, 0)),
                  pl.BlockSpec((1, nk + nq, 2 * nq), lambda bi, h, i: (h, 0, 0)),
                  pl.BlockSpec(lamv.shape, lambda bi, h, i: (0, 0)),
                  pl.BlockSpec(subln_col.shape, lambda bi, h, i: (0, 0))],
        out_specs=pl.BlockSpec((1, nq, HEAD_W), lambda bi, h, i: (bi, i, h)),
        scratch_shapes=[pltpu.VMEM((nk, 2 * nq), F32), pltpu.VMEM((nk, 2 * nq), BF16),
                        pltpu.VMEM((D_V, 2 * nq), F32)],
        compiler_params=_params(("parallel", "parallel", "arbitrary")),
        name="prompt_attention",
    )(qt, kb, vt, bias_tiles, lamv, subln_col)


def _sattn_kernel(pt_ref, q_ref, *refs, lam_init, n_steps):
    del pt_ref
    npg = PAGES_PER_STEP
    k_refs, v_refs = refs[:npg], refs[npg:2 * npg]
    (kn_ref, vn_ref, masks_ref, mnew_ref, lamv_ref, subln_ref, o_ref, m_ref, l_ref, acc_ref) = refs[2 * npg:]
    c = pl.program_id(1)
    page_rows = k_refs[0].shape[1]

    @pl.when(c == 0)
    def _():
        m_ref[...] = jnp.full(m_ref.shape, NEG, F32)
        l_ref[...] = jnp.zeros(l_ref.shape, F32)
        acc_ref[...] = jnp.zeros(acc_ref.shape, F32)

    q = q_ref[0]
    far_mask = masks_ref[0]
    last_mask = masks_ref[jnp.where(c == n_steps - 1, 1, 0)]
    s_pages = [_dot_nt(q, k_refs[p][0].astype(BF16)) + (far_mask if p < npg - 1 else last_mask)
               for p in range(npg)]
    s = jnp.concatenate(s_pages, axis=1)
    m = m_ref[...]
    m_new = jnp.maximum(m, jnp.max(s, axis=-1, keepdims=True))
    alpha = jnp.exp2(m - m_new)
    p = jnp.exp2(s - m_new)
    l_ref[...] = alpha * l_ref[...] + jnp.sum(p, axis=-1, keepdims=True)
    pb = p.astype(BF16)
    pv = _dot(pb[:, :page_rows], v_refs[0][0].astype(BF16))
    for pg in range(1, npg):
        pv = pv + _dot(pb[:, pg * page_rows:(pg + 1) * page_rows], v_refs[pg][0].astype(BF16))
    acc_ref[...] = alpha * acc_ref[...] + pv
    m_ref[...] = m_new

    @pl.when(c == n_steps - 1)
    def _():
        s_new = _dot_nt(q, kn_ref[0]) + mnew_ref[...]
        m1 = m_ref[...]
        m2 = jnp.maximum(m1, jnp.max(s_new, axis=-1, keepdims=True))
        a2 = jnp.exp2(m1 - m2)
        p2 = jnp.exp2(s_new - m2)
        l2 = a2 * l_ref[...] + jnp.sum(p2, axis=-1, keepdims=True)
        o = (a2 * acc_ref[...] + _dot(p2.astype(BF16), vn_ref[0])) / l2
        lam = _lambda_value(lamv_ref, lam_init)
        rows = o.shape[0] // H_DIFF
        nq = rows // 2
        for h in range(H_DIFF):
            oh = o[h * rows:(h + 1) * rows]
            o_ref[0, :, h * D_V:(h + 1) * D_V] = _head_out(oh[:nq], oh[nq:], lam, subln_ref[...], lam_init
                                                           ).astype(o_ref.dtype)


def _sample_attention(q_rows, cache_k, cache_v, page_table, k_new, v_new, masks, mask_new, lamv, subln,
                      lam_init, nq):
    bd, n_pages = page_table.shape
    npg = PAGES_PER_STEP
    n_steps = n_pages // npg
    rows = q_rows.shape[1]
    page_rows = cache_k.shape[1]

    def page_spec(p):
        return pl.BlockSpec((1, page_rows, HEAD_W), lambda b, c, pt: (pt[b, c * npg + p], 0, 0))

    per_batch = lambda shape: pl.BlockSpec((1,) + shape, lambda b, c, pt: (b, 0, 0))
    const = lambda shape: pl.BlockSpec(shape, lambda b, c, pt: (0,) * len(shape))
    grid_spec = pltpu.PrefetchScalarGridSpec(
        num_scalar_prefetch=1,
        grid=(bd, n_steps),
        in_specs=[per_batch((rows, HEAD_W))]
        + [page_spec(p) for p in range(npg)] * 2
        + [per_batch(k_new.shape[1:]), per_batch(v_new.shape[1:]),
           const(masks.shape), const(mask_new.shape), const(lamv.shape), const(subln.shape)],
        out_specs=per_batch((nq, H_DIFF * D_V)),
        scratch_shapes=[pltpu.VMEM((rows, 1), F32), pltpu.VMEM((rows, 1), F32), pltpu.VMEM((rows, D_V), F32)],
    )
    return pl.pallas_call(
        functools.partial(_sattn_kernel, lam_init=lam_init, n_steps=n_steps),
        out_shape=jax.ShapeDtypeStruct((bd, nq, H_DIFF * D_V), BF16),
        grid_spec=grid_spec,
        compiler_params=_params(("parallel", "arbitrary")),
        name="sample_attention",
    )(page_table, q_rows, *([cache_k] * npg), *([cache_v] * npg), k_new, v_new, masks, mask_new, lamv, subln)


def _ln_silu(y, g, b):
    mu = jnp.mean(y, axis=-1, keepdims=True)
    yc = y - mu
    var = jnp.mean(yc * yc, axis=-1, keepdims=True)
    z = yc * lax.rsqrt(var + EPS) * g + b
    return z * jax.nn.sigmoid(z)


def _conv_prompt_kernel(cur_ref, halo_ref, dw_ref, db_ref, lg_ref, lb_ref, o_ref, win_ref):
    i = pl.program_id(1)
    keep = (i > 0).astype(F32)
    win_ref[0, 0:CONV_HALO, :] = halo_ref[0] * keep
    win_ref[0, CONV_HALO:, :] = cur_ref[0]
    first = CONV_HALO - (CONV_W - 1)
    span = CONV_TILE + max((first + w) // SUBLANES * SUBLANES for w in range(CONV_W) if (first + w) % SUBLANES)
    assert SUBLANES - 1 + span <= CONV_HALO + CONV_TILE
    for s in range(1, SUBLANES):
        win_ref[s, 0:span, :] = win_ref[0, s:s + span, :]

    for r0 in range(0, CONV_TILE, CONV_ROWS):
        acc = None
        for w in range(CONV_W):
            base, s = (first + w) // SUBLANES * SUBLANES, (first + w) % SUBLANES
            term = win_ref[s, r0 + base:r0 + base + CONV_ROWS, :] * dw_ref[w:w + 1, :]
            acc = term if acc is None else acc + term
        y = _ln_silu(acc + db_ref[...], lg_ref[...], lb_ref[...])
        o_ref[0, r0:r0 + CONV_ROWS, :] = y.astype(o_ref.dtype)


def _conv_prompt(a, dw, db, lg, lb):
    b, t, c = a.shape
    per_tile = CONV_TILE // CONV_HALO
    cur = pl.BlockSpec((1, CONV_TILE, c), lambda bi, i: (bi, i, 0))
    halo = pl.BlockSpec((1, CONV_HALO, c), lambda bi, i: (bi, jnp.maximum(i * per_tile - 1, 0), 0))
    return pl.pallas_call(
        _conv_prompt_kernel,
        out_shape=jax.ShapeDtypeStruct((b, t, c), BF16),
        grid=(b, t // CONV_TILE),
        in_specs=[cur, halo, _resident(dw.shape), _resident(db.shape), _resident(lg.shape), _resident(lb.shape)],
        out_specs=cur,
        scratch_shapes=[pltpu.VMEM((SUBLANES, CONV_HALO + CONV_TILE, c), F32)],
        compiler_params=_params(("parallel", "parallel")),
        name="conv_prompt",
    )(a, a, dw, db, lg, lb)


def _conv_sample_kernel(pad_ref, dw_ref, db_ref, lg_ref, lb_ref, o_ref):
    nb, nt = o_ref.shape[0], o_ref.shape[1]
    for bb in range(nb):
        acc = pad_ref[bb, 0:nt, :] * dw_ref[0:1, :]
        for w in range(1, CONV_W):
            acc = acc + pad_ref[bb, w:w + nt, :] * dw_ref[w:w + 1, :]
        y = _ln_silu(acc + db_ref[...], lg_ref[...], lb_ref[...])
        o_ref[bb] = y.astype(o_ref.dtype)


def _conv_sample(pad, dw, db, lg, lb, nb=8):
    bd, tp, c = pad.shape
    nt = tp - (CONV_W - 1)
    return pl.pallas_call(
        _conv_sample_kernel,
        out_shape=jax.ShapeDtypeStruct((bd, nt, c), BF16),
        grid=(bd // nb,),
        in_specs=[pl.BlockSpec((nb, tp, c), lambda i: (i, 0, 0)),
                  _resident(dw.shape), _resident(db.shape), _resident(lg.shape), _resident(lb.shape)],
        out_specs=pl.BlockSpec((nb, nt, c), lambda i: (i, 0, 0)),
        compiler_params=_params(("parallel",)),
        name="conv_sample",
    )(pad, dw, db, lg, lb)


def _mixout_kernel(h_ref, att_ref, cnv_ref, ga_ref, gc_ref, wa_ref, wc_ref, wo_ref, o_ref):
    att = _dot(att_ref[...], wa_ref[...])
    cnv = _dot(cnv_ref[...], wc_ref[...])
    mix = (ga_ref[...] * att + gc_ref[...] * cnv).astype(BF16)
    o_ref[...] = h_ref[...] + _dot(mix, wo_ref[...])


def _mixout(h, att, cnv, ga, gc, wa, wc, wo):
    n = h.shape[0]
    tile = _row_tile(n)
    row = pl.BlockSpec((tile, D_MODEL), lambda i: (i, 0))
    return pl.pallas_call(
        _mixout_kernel,
        out_shape=jax.ShapeDtypeStruct((n, D_MODEL), F32),
        grid=(n // tile,),
        in_specs=[row] * 5 + [_resident(wa.shape), _resident(wc.shape), _resident(wo.shape)],
        out_specs=row,
        compiler_params=_params(("parallel",)),
        name="mixout",
    )(h, att, cnv, ga, gc, wa, wc, wo)


def _memkv_kernel(m_ref, g_ref, w_ref, k_ref, v_ref):
    u = _rms(m_ref[...], g_ref[...]).astype(BF16)
    half = X_HEADS * X_DH
    k_ref[...] = _dot(u, w_ref[:, :half])
    v_ref[...] = _dot(u, w_ref[:, half:])


def _memkv(mem, g, w):
    n = mem.shape[0]
    tile = _row_tile(n)
    half = X_HEADS * X_DH
    out = pl.BlockSpec((tile, half), lambda i: (i, 0))
    o = jax.ShapeDtypeStruct((n, half), F32)
    return pl.pallas_call(
        _memkv_kernel,
        out_shape=(o, o),
        grid=(n // tile,),
        in_specs=[pl.BlockSpec((tile, D_MODEL), lambda i: (i, 0)), _resident(g.shape), _resident(w.shape)],
        out_specs=(out, out),
        compiler_params=_params(("parallel",)),
        name="mem_kv",
    )(mem, g, w)


def _cross_kernel(h_ref, g_ref, wq_ref, mk_ref, mv_ref, wo_ref, o_ref, *, rows_per_batch):
    h = h_ref[...]
    u = _rms(h, g_ref[...]).astype(BF16)
    q = (_dot(u, wq_ref[...]) * X_SCALE).astype(BF16)
    nb, n_mem, width = mk_ref.shape
    mk = mk_ref[...].reshape(nb * n_mem, width).astype(BF16)
    mv = mv_ref[...].reshape(nb * n_mem, width).astype(BF16)
    rows = h.shape[0]
    if nb > 1:
        rb = lax.broadcasted_iota(jnp.int32, (rows, nb * n_mem), 0) // rows_per_batch
        cb = lax.broadcasted_iota(jnp.int32, (rows, nb * n_mem), 1) // n_mem
        own = rb == cb
    outs = []
    for hd in range(X_HEADS):
        sl = slice(hd * X_DH, (hd + 1) * X_DH)
        s = _dot_nt(q[:, sl], mk[:, sl])
        if nb > 1:
            s = jnp.where(own, s, NEG)
        p = jnp.exp(s - jnp.max(s, axis=-1, keepdims=True))
        o = _dot(p.astype(BF16), mv[:, sl]) / jnp.sum(p, axis=-1, keepdims=True)
        outs.append(o.astype(BF16))
    o_ref[...] = h + _dot(jnp.concatenate(outs, axis=1), wo_ref[...])


def _cross(h, g, wq, mk, mv, wo, rows_per_batch):
    n = h.shape[0]
    tile = _row_tile(n)
    if rows_per_batch >= tile:
        nb = 1
        tiles_per_batch = rows_per_batch // tile
        mem_map = lambda i: (i // tiles_per_batch, 0, 0)
    else:
        nb = 8
        tile = nb * rows_per_batch
        mem_map = lambda i: (i, 0, 0)
    row = pl.BlockSpec((tile, D_MODEL), lambda i: (i, 0))
    mem = pl.BlockSpec((nb,) + mk.shape[1:], mem_map)
    return pl.pallas_call(
        functools.partial(_cross_kernel, rows_per_batch=rows_per_batch),
        out_shape=jax.ShapeDtypeStruct((n, D_MODEL), F32),
        grid=(n // tile,),
        in_specs=[row, _resident(g.shape), _resident(wq.shape), mem, mem, _resident(wo.shape)],
        out_specs=row,
        compiler_params=_params(("parallel",)),
        name="cross_attention",
    )(h, g, wq, mk, mv, wo)


def _t5_bucket(rel):
    n = jnp.maximum(rel, 0)
    max_exact = N_BUCKETS // 2
    log_ratio = jnp.log(jnp.maximum(n, 1).astype(F32) / max_exact) / math.log(MAX_DIST / max_exact)
    large = jnp.minimum(max_exact + (log_ratio * (N_BUCKETS - max_exact)).astype(jnp.int32), N_BUCKETS - 1)
    return jnp.where(n < max_exact, n, large)


def _bias_minus_far(table, rel, visible):
    b = jnp.moveaxis(table[_t5_bucket(rel)], -1, 0).astype(F32) - table[N_BUCKETS - 1][:, None, None]
    return jnp.where(visible[None], b, NEG)


def _prompt_bias_tiles(table):
    nq, nk = Q_BLKS * ATT_BLK, K_BLKS * ATT_BLK
    rel = jnp.arange(nq)[None, :] + nk - jnp.arange(nk + nq)[:, None]
    bucket = _t5_bucket(rel)[None]
    by_bucket = (table - table[N_BUCKETS - 1]) * LOG2E
    tile = jnp.zeros((H_DIFF,) + rel.shape, F32)
    for b in range(N_BUCKETS - 1):
        tile = jnp.where(bucket == b, by_bucket[b][:, None, None], tile)
    return jnp.tile(jnp.where((rel >= 0)[None], tile, NEG), (1, 1, 2))


def kernel(x_prompt, x_sample, mem_prompt, cache_k, cache_v, page_table, state_conv, cache_mem_k, cache_mem_v,
           rel_bias_table, norm_ffn1, ffn1_w_in, ffn1_w_out, norm_mix, w_in, lambda_q1, lambda_k1, lambda_q2,
           lambda_k2, subln, w_attn_o, conv_dw_w, conv_dw_b, conv_ln_g, conv_ln_b, w_conv_o, w_out, norm_cross,
           norm_mem, w_xq, w_xkv, w_xo, norm_ffn2, ffn2_w_in, ffn2_w_out, norm_final):
    depth = norm_ffn1.shape[0]
    assert depth == 1, "single-layer step"
    l = 0
    lam_init = 0.8 - 0.6 * math.exp(-0.3 * l)
    b, t, d = x_prompt.shape
    bd, tn, _ = x_sample.shape
    n_pages = page_table.shape[1]
    n_mem = mem_prompt.shape[1]
    assert d == D_MODEL and t % (Q_BLKS * ATT_BLK) == 0 and t % CONV_TILE == 0 and ATT_BLK > MAX_DIST
    assert n_pages % PAGES_PER_STEP == 0 and PAGE_SIZE >= MAX_DIST and bd % 8 == 0

    row = lambda v: v.reshape(1, -1).astype(F32)
    bf = lambda w: w.astype(BF16)
    g_ffn1, g_mix, g_cross, g_mem, g_ffn2, g_fin = (row(norm_ffn1[l]), row(norm_mix[l]), row(norm_cross[l]),
                                                    row(norm_mem[l]), row(norm_ffn2[l]), row(norm_final))
    w1g, w1u, w1o = bf(ffn1_w_in[l][:, :D_FF]), bf(ffn1_w_in[l][:, D_FF:]), bf(ffn1_w_out[l])
    w2g, w2u, w2o = bf(ffn2_w_in[l][:, :D_FF]), bf(ffn2_w_in[l][:, D_FF:]), bf(ffn2_w_out[l])
    w_mix, w_ao, w_co, w_o = bf(w_in[l]), bf(w_attn_o[l]), bf(w_conv_o[l]), bf(w_out[l])
    wqt, wvt = w_mix[:, :d].T, w_mix[:, 2 * d:3 * d].T
    wq, wkv, wxo = bf(w_xq[l]), bf(w_xkv[l]), bf(w_xo[l])
    lamv = jnp.stack([lambda_q1[l], lambda_k1[l], lambda_q2[l], lambda_k2[l]]).astype(F32)
    sub = row(subln[l])
    dw, db, lg, lb = conv_dw_w[l].astype(F32), row(conv_dw_b[l]), row(conv_ln_g[l]), row(conv_ln_b[l])
    table = rel_bias_table.astype(F32)

    blk = ATT_BLK
    bias_tiles = _prompt_bias_tiles(table)

    xp = x_prompt.reshape(b * t, d)
    hp = _ffn(xp, g_ffn1, w1g, w1u, w1o, g_fin, False)
    qpt, kp, vp, kpb, vpt, ap, gap, gcp = _mixproj(hp, g_mix, w_mix, wqt, wvt, True)
    att_p = _prompt_attention(qpt.reshape(b, t // blk, d, blk), kpb.reshape(b, t, d), vpt.reshape(b, t // blk, d, blk),
                              bias_tiles, lamv, sub.reshape(-1, 1), lam_init)
    cnv_p = _conv_prompt(ap.reshape(b, t, d), dw, db, lg, lb)
    hp = _mixout(hp, att_p.reshape(b * t, d), cnv_p.reshape(b * t, d), gap, gcp, w_ao, w_co, w_o)
    mkp, mvp = _memkv(mem_prompt.reshape(b * n_mem, d), g_mem, wkv)
    hp = _cross(hp, g_cross, wq, mkp.reshape(b, n_mem, -1), mvp.reshape(b, n_mem, -1), wxo, t)
    yp = _ffn(hp, g_ffn2, w2g, w2u, w2o, g_fin, True)

    xs = x_sample.reshape(bd * tn, d)
    hs = _ffn(xs, g_ffn1, w1g, w1u, w1o, g_fin, False)
    qs, ks, vs, ksb, vsb, as_, gas, gcs = _mixproj(hs, g_mix, w_mix, wqt, wvt, False)
    q5 = jnp.transpose(qs.reshape(bd, tn, H_DIFF, 2, D_QK), (0, 2, 1, 3, 4))[:, :, None]
    own_map = jnp.eye(2, dtype=bool)[None, None, :, None, :, None]
    q_rows = jnp.where(own_map, q5, 0).reshape(bd, H_DIFF * 2 * tn, HEAD_W)
    def key_mask(bias):
        full = jnp.broadcast_to(bias[:, None, :, :, None], (H_DIFF, 2) + bias.shape[1:] + (H_DIFF,))
        own_head = jnp.arange(H_DIFF)[:, None, None, None, None] == jnp.arange(H_DIFF)[None, None, None, None, :]
        return jnp.where(own_head, full * LOG2E, NEG).reshape(H_DIFF * 2 * bias.shape[1], -1)
    tok = jnp.arange(tn)[:, None]
    rel_last = PAGE_SIZE + tok - jnp.arange(PAGE_SIZE)[None, :]
    masks = jnp.stack([key_mask(jnp.zeros((H_DIFF, tn, PAGE_SIZE), F32)),
                       key_mask(_bias_minus_far(table, rel_last, rel_last > 0))])
    mask_new = key_mask(_bias_minus_far(table, tok - tok.T, tok.T <= tok))
    as_pages = lambda cache: cache.reshape(-1, PAGE_SIZE * H_DIFF, HEAD_W)
    att_s = _sample_attention(q_rows, as_pages(cache_k), as_pages(cache_v), page_table,
                              ksb.reshape(bd, tn * H_DIFF, HEAD_W), vsb.reshape(bd, tn * H_DIFF, D_V),
                              masks, mask_new, lamv, sub, lam_init, tn)
    pad_s = jnp.concatenate([state_conv[l].astype(F32), as_.reshape(bd, tn, d)], axis=1)
    cnv_s = _conv_sample(pad_s, dw, db, lg, lb)
    hs = _mixout(hs, att_s.reshape(bd * tn, d), cnv_s.reshape(bd * tn, d), gas, gcs, w_ao, w_co, w_o)
    hs = _cross(hs, g_cross, wq, cache_mem_k.reshape(bd, n_mem, -1), cache_mem_v.reshape(bd, n_mem, -1), wxo, tn)
    ys = _ffn(hs, g_ffn2, w2g, w2u, w2o, g_fin, True)

    return (yp.reshape(b, t, d), ys.reshape(bd, tn, d),
            kp.reshape(1, b, t, H_DIFF, HEAD_W), vp.reshape(1, b, t, H_DIFF, D_V),
            ap.reshape(b, t, d)[None, :, t - (CONV_W - 1):],
            mkp.reshape(1, b, n_mem, X_HEADS, X_DH), mvp.reshape(1, b, n_mem, X_HEADS, X_DH),
            ks.reshape(1, bd, tn, H_DIFF, HEAD_W), vs.reshape(1, bd, tn, H_DIFF, D_V),
            pad_s[None, :, tn:])
```

```python
import functools
import math

import jax
import jax.numpy as jnp
from jax import lax
from jax.experimental import pallas as pl
from jax.experimental.pallas import tpu as pltpu

F32 = jnp.float32
BF16 = jnp.bfloat16

D_MODEL = 1024
H_DIFF = 8
D_QK = 64
D_V = 128
HEAD_W = 2 * D_QK
CONV_CH = D_MODEL
CONV_W = 31
D_FF = 2816
X_HEADS = 4
X_DH = 128
N_BUCKETS = 32
MAX_DIST = 128
PAGE_SIZE = 128
EPS = 1e-6
NEG = -1e30
QK_SCALE = D_QK ** -0.5
X_SCALE = X_DH ** -0.5
LOG2E = math.log2(math.e)

SUBLANES = 8
V7X_VMEM_BYTES = 64 * 1024 * 1024
VMEM_LIMIT = 56 * 1024 * 1024

ROW_TILE = 256
FFN_TILE = 512
ATT_BLK = 256
Q_BLKS = 2
K_BLKS = 2
HEADS_PER_STEP = 2
CONV_TILE = 256
CONV_HALO = 32
CONV_ROWS = 32
PAGES_PER_STEP = 8


def _params(sem):
    return pltpu.CompilerParams(dimension_semantics=sem, vmem_limit_bytes=VMEM_LIMIT)


def _resident(shape):
    nd = len(shape)
    return pl.BlockSpec(shape, lambda *_: (0,) * nd, pipeline_mode=pl.Buffered(1))


def _rms(x, g):
    return x * lax.rsqrt(jnp.mean(x * x, axis=-1, keepdims=True) + EPS) * g


def _dot(a, b):
    return jnp.dot(a, b, preferred_element_type=F32)


def _dot_nt(a, b):
    return lax.dot_general(a, b, (((1,), (1,)), ((), ())), preferred_element_type=F32)


def _ffn_kernel(x_ref, g_ref, wg_ref, wu_ref, wo_ref, gf_ref, o_ref, *, final_norm):
    x = x_ref[...]
    u = _rms(x, g_ref[...]).astype(BF16)
    gate = _dot(u, wg_ref[...])
    up = _dot(u, wu_ref[...])
    act = (gate * jax.nn.sigmoid(gate) * up).astype(BF16)
    h = x + 0.5 * _dot(act, wo_ref[...])
    if final_norm:
        h = _rms(h, gf_ref[...])
    o_ref[...] = h


def _row_tile(n, rows=ROW_TILE):
    tile = min(rows, n)
    assert n % tile == 0 and tile % 8 == 0
    return tile


def _ffn(x, g, wg, wu, wo, gf, final_norm):
    n = x.shape[0]
    tile = _row_tile(n, FFN_TILE)
    row = pl.BlockSpec((tile, D_MODEL), lambda i: (i, 0))
    return pl.pallas_call(
        functools.partial(_ffn_kernel, final_norm=final_norm),
        out_shape=jax.ShapeDtypeStruct((n, D_MODEL), F32),
        grid=(n // tile,),
        in_specs=[row, _resident(g.shape), _resident(wg.shape), _resident(wu.shape),
                  _resident(wo.shape), _resident(gf.shape)],
        out_specs=row,
        compiler_params=_params(("parallel",)),
        name="ffn",
    )(x, g, wg, wu, wo, gf)


def _mixproj_kernel(h_ref, g_ref, w_ref, wqt_ref, wvt_ref, q_ref, k_ref, v_ref, kb_ref, vb_ref, a_ref, ga_ref,
                    gc_ref, *, transposed):
    u = _rms(h_ref[...], g_ref[...]).astype(BF16)
    d = D_MODEL

    def cols(j):
        return _dot(u, w_ref[:, j * d:(j + 1) * d])

    k = cols(1)
    k_ref[...] = k
    kb_ref[...] = k.astype(BF16)
    v = cols(2)
    v_ref[...] = v
    if transposed:
        q_ref[0] = (_dot_nt(wqt_ref[...], u) * (QK_SCALE * LOG2E)).astype(BF16)
        vb_ref[0] = _dot_nt(wvt_ref[...], u).astype(BF16)
    else:
        q_ref[...] = (cols(0) * (QK_SCALE * LOG2E)).astype(BF16)
        vb_ref[...] = v.astype(BF16)
    a_ref[...] = cols(3) * jax.nn.sigmoid(cols(4))
    ga_ref[...] = jax.nn.sigmoid(cols(5))
    gc_ref[...] = jax.nn.sigmoid(cols(6))


def _mixproj(h, g, w, wqt, wvt, transposed):
    n = h.shape[0]
    tile = ATT_BLK if transposed else _row_tile(n)
    row = pl.BlockSpec((tile, D_MODEL), lambda i: (i, 0))
    f32o = jax.ShapeDtypeStruct((n, D_MODEL), F32)
    b16o = jax.ShapeDtypeStruct((n, D_MODEL), BF16)
    if transposed:
        t16o = jax.ShapeDtypeStruct((n // tile, D_MODEL, tile), BF16)
        trow = pl.BlockSpec((1, D_MODEL, tile), lambda i: (i, 0, 0))
    else:
        t16o, trow = b16o, row
    return pl.pallas_call(
        functools.partial(_mixproj_kernel, transposed=transposed),
        out_shape=(t16o, f32o, f32o, b16o, t16o, f32o, f32o, f32o),
        grid=(n // tile,),
        in_specs=[row, _resident(g.shape), _resident(w.shape), _resident(wqt.shape), _resident(wvt.shape)],
        out_specs=(trow, row, row, row, trow, row, row, row),
        compiler_params=_params(("parallel",)),
        name="mixproj",
    )(h, g, w, wqt, wvt)


def _lambda_value(lamv_ref, lam_init):
    lv = lamv_ref[...]
    d1 = jnp.sum(lv[0:1] * lv[1:2], axis=-1, keepdims=True)
    d2 = jnp.sum(lv[2:3] * lv[3:4], axis=-1, keepdims=True)
    return jnp.exp(d1) - jnp.exp(d2) + lam_init


def _head_out(o1, o2, lam, subln, lam_init):
    o = o1 - lam * o2
    return o * lax.rsqrt(jnp.mean(o * o, axis=-1, keepdims=True) + EPS) * subln * (1.0 - lam_init)


def _pattn_kernel(qt_ref, k_ref, vt_ref, bias_ref, lamv_ref, subln_ref, o_ref, s_ref, p_ref, acc_ref, *,
                  lam_init):
    i = pl.program_id(2)
    blk = ATT_BLK
    nq, nk = Q_BLKS * blk, K_BLKS * blk
    first = (nq // nk) * i
    n_far = jnp.maximum(first - 1, 0)
    n_loop = jnp.maximum(n_far - 1, 0)
    j_near = jnp.maximum(first - 1, 0)
    heads = range(HEADS_PER_STEP)
    lanes = [slice(h * HEAD_W, (h + 1) * HEAD_W) for h in heads]

    def qq_t(h):
        qt = jnp.concatenate([qt_ref[0, c, lanes[h], :] for c in range(Q_BLKS)], axis=1)
        row = lax.broadcasted_iota(jnp.int32, qt.shape, 0)
        zero = jnp.zeros_like(qt)
        return jnp.concatenate([jnp.where(row < D_QK, qt, zero), jnp.where(row >= D_QK, qt, zero)], axis=1)

    qqt = [qq_t(h) for h in heads]

    def scores(h, j, n=nk):
        return _dot(k_ref[0, pl.ds(pl.multiple_of(j * nk, nk), n), lanes[h]], qqt[h])

    def values(h, j, p):
        out = _dot(vt_ref[0, K_BLKS * j, lanes[h], :], p[:blk])
        for c in range(1, p.shape[0] // blk):
            out = out + _dot(vt_ref[0, K_BLKS * j + c, lanes[h], :], p[c * blk:(c + 1) * blk])
        return out

    def softmax(s, m, l):
        m_new = jnp.maximum(m, jnp.max(s, axis=0, keepdims=True))
        alpha = jnp.exp2(m - m_new)
        p = jnp.exp2(s - m_new)
        return m_new, alpha * l + jnp.sum(p, axis=0, keepdims=True), alpha, p.astype(BF16)

    def absent(flag):
        return jnp.where(flag, 2.0 * NEG, 0.0).astype(F32)

    def head_start(h):
        s_ref[h] = scores(h, 0)
        init = (jnp.full((1, 2 * nq), NEG, F32), jnp.zeros((1, 2 * nq), F32))
        m, l, _, p = softmax(scores(h, first, nq) + bias_ref[h, nk:, :], *init)
        acc = values(h, first, p)
        m, l, alpha, p = softmax(scores(h, j_near) + (bias_ref[h, 0:nk, :] + absent(first == 0)), m, l)
        acc_ref[h] = alpha * acc + values(h, j_near, p)
        p_ref[h] = jnp.zeros(p_ref.shape[1:], p_ref.dtype)
        return m, l, jnp.ones((1, 2 * nq), F32)

    def far_step(j, carry):
        out = []
        for h in heads:
            m, l, alpha_prev = carry[h]
            acc_ref[h] = alpha_prev * acc_ref[h] + values(h, jnp.maximum(j - 1, 0), p_ref[h])
            m, l, alpha, p = softmax(s_ref[h], m, l)
            p_ref[h] = p
            s_ref[h] = scores(h, j + 1)
            out.append((m, l, alpha))
        return tuple(out)

    def head_finish(h, m, l, alpha_prev):
        acc = alpha_prev * acc_ref[h] + values(h, jnp.maximum(n_loop - 1, 0), p_ref[h])
        m, l, alpha, p = softmax(s_ref[h] + absent(n_far == 0), m, l)
        o = (alpha * acc + values(h, n_loop, p)) / l
        o = o[:, :nq] - _lambda_value(lamv_ref, lam_init) * o[:, nq:]
        o = o * lax.rsqrt(jnp.mean(o * o, axis=0, keepdims=True) + EPS) * subln_ref[...] * (1.0 - lam_init)
        o_ref[0, :, lanes[h]] = o.T.astype(o_ref.dtype)

    carry = lax.fori_loop(0, n_loop, far_step, tuple(head_start(h) for h in heads))
    for h in heads:
        head_finish(h, *carry[h])


def _prompt_attention(qt, kb, vt, bias_tiles, lamv, subln_col, lam_init):
    b, t, _ = kb.shape
    blk = ATT_BLK
    nblk = t // blk
    nq, nk = Q_BLKS * blk, K_BLKS * blk
    hps = HEADS_PER_STEP
    assert nq % nk == 0 and t % nq == 0 and H_DIFF % hps == 0
    return pl.pallas_call(
        functools.partial(_pattn_kernel, lam_init=lam_init),
        out_shape=jax.ShapeDtypeStruct((b, t, H_DIFF * D_V), BF16),
        grid=(H_DIFF // hps, b, nblk // Q_BLKS),
        in_specs=[pl.BlockSpec((1, Q_BLKS, hps * HEAD_W, blk), lambda h, bi, i: (bi, i, h, 0)),
                  pl.BlockSpec((1, t, hps * HEAD_W), lambda h, bi, i: (bi, 0, h)),
                  pl.BlockSpec((1, nblk, hps * D_V, blk), lambda h, bi, i: (bi, 0, h, 0)),
                  pl.BlockSpec((hps, nk + nq, 2 * nq), lambda h, bi, i: (h, 0, 0), pipeline_mode=pl.Buffered(1)),
                  pl.BlockSpec(lamv.shape, lambda h, bi, i: (0, 0)),
                  pl.BlockSpec(subln_col.shape, lambda h, bi, i: (0, 0))],
        out_specs=pl.BlockSpec((1, nq, hps * HEAD_W), lambda h, bi, i: (bi, i, h)),
        scratch_shapes=[pltpu.VMEM((hps, nk, 2 * nq), F32), pltpu.VMEM((hps, nk, 2 * nq), BF16),
                        pltpu.VMEM((hps, D_V, 2 * nq), F32)],
        compiler_params=_params(("parallel", "parallel", "arbitrary")),
        name="prompt_attention",
    )(qt, kb, vt, bias_tiles, lamv, subln_col)


def _sattn_kernel(pt_ref, q_ref, *refs, lam_init, n_steps):
    del pt_ref
    npg = PAGES_PER_STEP
    k_refs, v_refs = refs[:npg], refs[npg:2 * npg]
    (kn_ref, vn_ref, masks_ref, mnew_ref, lamv_ref, subln_ref, o_ref, m_ref, l_ref, acc_ref) = refs[2 * npg:]
    c = pl.program_id(1)
    page_rows = k_refs[0].shape[1]

    @pl.when(c == 0)
    def _():
        m_ref[...] = jnp.full(m_ref.shape, NEG, F32)
        l_ref[...] = jnp.zeros(l_ref.shape, F32)
        acc_ref[...] = jnp.zeros(acc_ref.shape, F32)

    q = q_ref[0]
    far_mask = masks_ref[0]
    last_mask = masks_ref[jnp.where(c == n_steps - 1, 1, 0)]
    s_pages = [_dot_nt(q, k_refs[p][0].astype(BF16)) + (far_mask if p < npg - 1 else last_mask)
               for p in range(npg)]
    s = jnp.concatenate(s_pages, axis=1)
    m = m_ref[...]
    m_new = jnp.maximum(m, jnp.max(s, axis=-1, keepdims=True))
    alpha = jnp.exp2(m - m_new)
    p = jnp.exp2(s - m_new)
    l_ref[...] = alpha * l_ref[...] + jnp.sum(p, axis=-1, keepdims=True)
    pb = p.astype(BF16)
    pv = _dot(pb[:, :page_rows], v_refs[0][0].astype(BF16))
    for pg in range(1, npg):
        pv = pv + _dot(pb[:, pg * page_rows:(pg + 1) * page_rows], v_refs[pg][0].astype(BF16))
    acc_ref[...] = alpha * acc_ref[...] + pv
    m_ref[...] = m_new

    @pl.when(c == n_steps - 1)
    def _():
        s_new = _dot_nt(q, kn_ref[0]) + mnew_ref[...]
        m1 = m_ref[...]
        m2 = jnp.maximum(m1, jnp.max(s_new, axis=-1, keepdims=True))
        a2 = jnp.exp2(m1 - m2)
        p2 = jnp.exp2(s_new - m2)
        l2 = a2 * l_ref[...] + jnp.sum(p2, axis=-1, keepdims=True)
        o = (a2 * acc_ref[...] + _dot(p2.astype(BF16), vn_ref[0])) / l2
        lam = _lambda_value(lamv_ref, lam_init)
        rows = o.shape[0] // H_DIFF
        nq = rows // 2
        for h in range(H_DIFF):
            oh = o[h * rows:(h + 1) * rows]
            o_ref[0, :, h * D_V:(h + 1) * D_V] = _head_out(oh[:nq], oh[nq:], lam, subln_ref[...], lam_init
                                                           ).astype(o_ref.dtype)


def _sample_attention(q_rows, cache_k, cache_v, page_table, k_new, v_new, masks, mask_new, lamv, subln,
                      lam_init, nq):
    bd, n_pages = page_table.shape
    npg = PAGES_PER_STEP
    n_steps = n_pages // npg
    rows = q_rows.shape[1]
    page_rows = cache_k.shape[1]

    def page_spec(p):
        return pl.BlockSpec((1, page_rows, HEAD_W), lambda b, c, pt: (pt[b, c * npg + p], 0, 0))

    per_batch = lambda shape: pl.BlockSpec((1,) + shape, lambda b, c, pt: (b, 0, 0))
    const = lambda shape: pl.BlockSpec(shape, lambda b, c, pt: (0,) * len(shape))
    grid_spec = pltpu.PrefetchScalarGridSpec(
        num_scalar_prefetch=1,
        grid=(bd, n_steps),
        in_specs=[per_batch((rows, HEAD_W))]
        + [page_spec(p) for p in range(npg)] * 2
        + [per_batch(k_new.shape[1:]), per_batch(v_new.shape[1:]),
           const(masks.shape), const(mask_new.shape), const(lamv.shape), const(subln.shape)],
        out_specs=per_batch((nq, H_DIFF * D_V)),
        scratch_shapes=[pltpu.VMEM((rows, 1), F32), pltpu.VMEM((rows, 1), F32), pltpu.VMEM((rows, D_V), F32)],
    )
    return pl.pallas_call(
        functools.partial(_sattn_kernel, lam_init=lam_init, n_steps=n_steps),
        out_shape=jax.ShapeDtypeStruct((bd, nq, H_DIFF * D_V), BF16),
        grid_spec=grid_spec,
        compiler_params=_params(("parallel", "arbitrary")),
        name="sample_attention",
    )(page_table, q_rows, *([cache_k] * npg), *([cache_v] * npg), k_new, v_new, masks, mask_new, lamv, subln)


def _ln_silu(y, g, b):
    mu = jnp.mean(y, axis=-1, keepdims=True)
    yc = y - mu
    var = jnp.mean(yc * yc, axis=-1, keepdims=True)
    z = yc * lax.rsqrt(var + EPS) * g + b
    return z * jax.nn.sigmoid(z)


def _conv_prompt_kernel(cur_ref, halo_ref, dw_ref, db_ref, lg_ref, lb_ref, o_ref, win_ref):
    i = pl.program_id(1)
    keep = (i > 0).astype(F32)
    win_ref[0, 0:CONV_HALO, :] = halo_ref[0] * keep
    win_ref[0, CONV_HALO:, :] = cur_ref[0]
    first = CONV_HALO - (CONV_W - 1)
    span = CONV_TILE + max((first + w) // SUBLANES * SUBLANES for w in range(CONV_W) if (first + w) % SUBLANES)
    assert SUBLANES - 1 + span <= CONV_HALO + CONV_TILE
    for s in range(1, SUBLANES):
        win_ref[s, 0:span, :] = win_ref[0, s:s + span, :]

    for r0 in range(0, CONV_TILE, CONV_ROWS):
        acc = None
        for w in range(CONV_W):
            base, s = (first + w) // SUBLANES * SUBLANES, (first + w) % SUBLANES
            term = win_ref[s, r0 + base:r0 + base + CONV_ROWS, :] * dw_ref[w:w + 1, :]
            acc = term if acc is None else acc + term
        y = _ln_silu(acc + db_ref[...], lg_ref[...], lb_ref[...])
        o_ref[0, r0:r0 + CONV_ROWS, :] = y.astype(o_ref.dtype)


def _conv_prompt(a, dw, db, lg, lb):
    b, t, c = a.shape
    per_tile = CONV_TILE // CONV_HALO
    cur = pl.BlockSpec((1, CONV_TILE, c), lambda bi, i: (bi, i, 0))
    halo = pl.BlockSpec((1, CONV_HALO, c), lambda bi, i: (bi, jnp.maximum(i * per_tile - 1, 0), 0))
    return pl.pallas_call(
        _conv_prompt_kernel,
        out_shape=jax.ShapeDtypeStruct((b, t, c), BF16),
        grid=(b, t // CONV_TILE),
        in_specs=[cur, halo, _resident(dw.shape), _resident(db.shape), _resident(lg.shape), _resident(lb.shape)],
        out_specs=cur,
        scratch_shapes=[pltpu.VMEM((SUBLANES, CONV_HALO + CONV_TILE, c), F32)],
        compiler_params=_params(("parallel", "parallel")),
        name="conv_prompt",
    )(a, a, dw, db, lg, lb)


def _conv_sample_kernel(pad_ref, dw_ref, db_ref, lg_ref, lb_ref, o_ref):
    nb, nt = o_ref.shape[0], o_ref.shape[1]
    for bb in range(nb):
        acc = pad_ref[bb, 0:nt, :] * dw_ref[0:1, :]
        for w in range(1, CONV_W):
            acc = acc + pad_ref[bb, w:w + nt, :] * dw_ref[w:w + 1, :]
        y = _ln_silu(acc + db_ref[...], lg_ref[...], lb_ref[...])
        o_ref[bb] = y.astype(o_ref.dtype)


def _conv_sample(pad, dw, db, lg, lb, nb=8):
    bd, tp, c = pad.shape
    nt = tp - (CONV_W - 1)
    return pl.pallas_call(
        _conv_sample_kernel,
        out_shape=jax.ShapeDtypeStruct((bd, nt, c), BF16),
        grid=(bd // nb,),
        in_specs=[pl.BlockSpec((nb, tp, c), lambda i: (i, 0, 0)),
                  _resident(dw.shape), _resident(db.shape), _resident(lg.shape), _resident(lb.shape)],
        out_specs=pl.BlockSpec((nb, nt, c), lambda i: (i, 0, 0)),
        compiler_params=_params(("parallel",)),
        name="conv_sample",
    )(pad, dw, db, lg, lb)


def _mixout_kernel(h_ref, att_ref, cnv_ref, ga_ref, gc_ref, wa_ref, wc_ref, wo_ref, o_ref):
    att = _dot(att_ref[...], wa_ref[...])
    cnv = _dot(cnv_ref[...], wc_ref[...])
    mix = (ga_ref[...] * att + gc_ref[...] * cnv).astype(BF16)
    o_ref[...] = h_ref[...] + _dot(mix, wo_ref[...])


def _mixout(h, att, cnv, ga, gc, wa, wc, wo):
    n = h.shape[0]
    tile = _row_tile(n)
    row = pl.BlockSpec((tile, D_MODEL), lambda i: (i, 0))
    return pl.pallas_call(
        _mixout_kernel,
        out_shape=jax.ShapeDtypeStruct((n, D_MODEL), F32),
        grid=(n // tile,),
        in_specs=[row] * 5 + [_resident(wa.shape), _resident(wc.shape), _resident(wo.shape)],
        out_specs=row,
        compiler_params=_params(("parallel",)),
        name="mixout",
    )(h, att, cnv, ga, gc, wa, wc, wo)


def _memkv_kernel(m_ref, g_ref, w_ref, k_ref, v_ref):
    u = _rms(m_ref[...], g_ref[...]).astype(BF16)
    half = X_HEADS * X_DH
    k_ref[...] = _dot(u, w_ref[:, :half])
    v_ref[...] = _dot(u, w_ref[:, half:])


def _memkv(mem, g, w):
    n = mem.shape[0]
    tile = _row_tile(n)
    half = X_HEADS * X_DH
    out = pl.BlockSpec((tile, half), lambda i: (i, 0))
    o = jax.ShapeDtypeStruct((n, half), F32)
    return pl.pallas_call(
        _memkv_kernel,
        out_shape=(o, o),
        grid=(n // tile,),
        in_specs=[pl.BlockSpec((tile, D_MODEL), lambda i: (i, 0)), _resident(g.shape), _resident(w.shape)],
        out_specs=(out, out),
        compiler_params=_params(("parallel",)),
        name="mem_kv",
    )(mem, g, w)


def _cross_kernel(h_ref, g_ref, wq_ref, mk_ref, mv_ref, wo_ref, o_ref, *, rows_per_batch):
    h = h_ref[...]
    u = _rms(h, g_ref[...]).astype(BF16)
    q = (_dot(u, wq_ref[...]) * X_SCALE).astype(BF16)
    nb, n_mem, width = mk_ref.shape
    mk = mk_ref[...].reshape(nb * n_mem, width).astype(BF16)
    mv = mv_ref[...].reshape(nb * n_mem, width).astype(BF16)
    rows = h.shape[0]
    if nb > 1:
        rb = lax.broadcasted_iota(jnp.int32, (rows, nb * n_mem), 0) // rows_per_batch
        cb = lax.broadcasted_iota(jnp.int32, (rows, nb * n_mem), 1) // n_mem
        own = rb == cb
    outs = []
    for hd in range(X_HEADS):
        sl = slice(hd * X_DH, (hd + 1) * X_DH)
        s = _dot_nt(q[:, sl], mk[:, sl])
        if nb > 1:
            s = jnp.where(own, s, NEG)
        p = jnp.exp(s - jnp.max(s, axis=-1, keepdims=True))
        o = _dot(p.astype(BF16), mv[:, sl]) / jnp.sum(p, axis=-1, keepdims=True)
        outs.append(o.astype(BF16))
    o_ref[...] = h + _dot(jnp.concatenate(outs, axis=1), wo_ref[...])


def _cross(h, g, wq, mk, mv, wo, rows_per_batch):
    n = h.shape[0]
    tile = _row_tile(n)
    if rows_per_batch >= tile:
        nb = 1
        tiles_per_batch = rows_per_batch // tile
        mem_map = lambda i: (i // tiles_per_batch, 0, 0)
    else:
        nb = 8
        tile = nb * rows_per_batch
        mem_map = lambda i: (i, 0, 0)
    row = pl.BlockSpec((tile, D_MODEL), lambda i: (i, 0))
    mem = pl.BlockSpec((nb,) + mk.shape[1:], mem_map)
    return pl.pallas_call(
        functools.partial(_cross_kernel, rows_per_batch=rows_per_batch),
        out_shape=jax.ShapeDtypeStruct((n, D_MODEL), F32),
        grid=(n // tile,),
        in_specs=[row, _resident(g.shape), _resident(wq.shape), mem, mem, _resident(wo.shape)],
        out_specs=row,
        compiler_params=_params(("parallel",)),
        name="cross_attention",
    )(h, g, wq, mk, mv, wo)


def _t5_bucket(rel):
    n = jnp.maximum(rel, 0)
    max_exact = N_BUCKETS // 2
    log_ratio = jnp.log(jnp.maximum(n, 1).astype(F32) / max_exact) / math.log(MAX_DIST / max_exact)
    large = jnp.minimum(max_exact + (log_ratio * (N_BUCKETS - max_exact)).astype(jnp.int32), N_BUCKETS - 1)
    return jnp.where(n < max_exact, n, large)


def _bias_minus_far(table, rel, visible):
    b = jnp.moveaxis(table[_t5_bucket(rel)], -1, 0).astype(F32) - table[N_BUCKETS - 1][:, None, None]
    return jnp.where(visible[None], b, NEG)


def _prompt_bias_tiles(table):
    nq, nk = Q_BLKS * ATT_BLK, K_BLKS * ATT_BLK
    rel = jnp.arange(nq)[None, :] + nk - jnp.arange(nk + nq)[:, None]
    bucket = _t5_bucket(rel)[None]
    by_bucket = (table - table[N_BUCKETS - 1]) * LOG2E
    tile = jnp.zeros((H_DIFF,) + rel.shape, F32)
    for b in range(N_BUCKETS - 1):
        tile = jnp.where(bucket == b, by_bucket[b][:, None, None], tile)
    return jnp.tile(jnp.where((rel >= 0)[None], tile, NEG), (1, 1, 2))


def kernel(x_prompt, x_sample, mem_prompt, cache_k, cache_v, page_table, state_conv, cache_mem_k, cache_mem_v,
           rel_bias_table, norm_ffn1, ffn1_w_in, ffn1_w_out, norm_mix, w_in, lambda_q1, lambda_k1, lambda_q2,
           lambda_k2, subln, w_attn_o, conv_dw_w, conv_dw_b, conv_ln_g, conv_ln_b, w_conv_o, w_out, norm_cross,
           norm_mem, w_xq, w_xkv, w_xo, norm_ffn2, ffn2_w_in, ffn2_w_out, norm_final):
    depth = norm_ffn1.shape[0]
    assert depth == 1, "single-layer step"
    l = 0
    lam_init = 0.8 - 0.6 * math.exp(-0.3 * l)
    b, t, d = x_prompt.shape
    bd, tn, _ = x_sample.shape
    n_pages = page_table.shape[1]
    n_mem = mem_prompt.shape[1]
    assert d == D_MODEL and t % (Q_BLKS * ATT_BLK) == 0 and t % CONV_TILE == 0 and ATT_BLK > MAX_DIST
    assert n_pages % PAGES_PER_STEP == 0 and PAGE_SIZE >= MAX_DIST and bd % 8 == 0

    row = lambda v: v.reshape(1, -1).astype(F32)
    bf = lambda w: w.astype(BF16)
    g_ffn1, g_mix, g_cross, g_mem, g_ffn2, g_fin = (row(norm_ffn1[l]), row(norm_mix[l]), row(norm_cross[l]),
                                                    row(norm_mem[l]), row(norm_ffn2[l]), row(norm_final))
    w1g, w1u, w1o = bf(ffn1_w_in[l][:, :D_FF]), bf(ffn1_w_in[l][:, D_FF:]), bf(ffn1_w_out[l])
    w2g, w2u, w2o = bf(ffn2_w_in[l][:, :D_FF]), bf(ffn2_w_in[l][:, D_FF:]), bf(ffn2_w_out[l])
    w_mix, w_ao, w_co, w_o = bf(w_in[l]), bf(w_attn_o[l]), bf(w_conv_o[l]), bf(w_out[l])
    wqt, wvt = w_mix[:, :d].T, w_mix[:, 2 * d:3 * d].T
    wq, wkv, wxo = bf(w_xq[l]), bf(w_xkv[l]), bf(w_xo[l])
    lamv = jnp.stack([lambda_q1[l], lambda_k1[l], lambda_q2[l], lambda_k2[l]]).astype(F32)
    sub = row(subln[l])
    dw, db, lg, lb = conv_dw_w[l].astype(F32), row(conv_dw_b[l]), row(conv_ln_g[l]), row(conv_ln_b[l])
    table = rel_bias_table.astype(F32)

    blk = ATT_BLK
    bias_tiles = _prompt_bias_tiles(table)

    xp = x_prompt.reshape(b * t, d)
    hp = _ffn(xp, g_ffn1, w1g, w1u, w1o, g_fin, False)
    qpt, kp, vp, kpb, vpt, ap, gap, gcp = _mixproj(hp, g_mix, w_mix, wqt, wvt, True)
    att_p = _prompt_attention(qpt.reshape(b, t // blk, d, blk), kpb.reshape(b, t, d), vpt.reshape(b, t // blk, d, blk),
                              bias_tiles, lamv, sub.reshape(-1, 1), lam_init)
    cnv_p = _conv_prompt(ap.reshape(b, t, d), dw, db, lg, lb)
    hp = _mixout(hp, att_p.reshape(b * t, d), cnv_p.reshape(b * t, d), gap, gcp, w_ao, w_co, w_o)
    mkp, mvp = _memkv(mem_prompt.reshape(b * n_mem, d), g_mem, wkv)
    hp = _cross(hp, g_cross, wq, mkp.reshape(b, n_mem, -1), mvp.reshape(b, n_mem, -1), wxo, t)
    yp = _ffn(hp, g_ffn2, w2g, w2u, w2o, g_fin, True)

    xs = x_sample.reshape(bd * tn, d)
    hs = _ffn(xs, g_ffn1, w1g, w1u, w1o, g_fin, False)
    qs, ks, vs, ksb, vsb, as_, gas, gcs = _mixproj(hs, g_mix, w_mix, wqt, wvt, False)
    q5 = jnp.transpose(qs.reshape(bd, tn, H_DIFF, 2, D_QK), (0, 2, 1, 3, 4))[:, :, None]
    own_map = jnp.eye(2, dtype=bool)[None, None, :, None, :, None]
    q_rows = jnp.where(own_map, q5, 0).reshape(bd, H_DIFF * 2 * tn, HEAD_W)
    def key_mask(bias):
        full = jnp.broadcast_to(bias[:, None, :, :, None], (H_DIFF, 2) + bias.shape[1:] + (H_DIFF,))
        own_head = jnp.arange(H_DIFF)[:, None, None, None, None] == jnp.arange(H_DIFF)[None, None, None, None, :]
        return jnp.where(own_head, full * LOG2E, NEG).reshape(H_DIFF * 2 * bias.shape[1], -1)
    tok = jnp.arange(tn)[:, None]
    rel_last = PAGE_SIZE + tok - jnp.arange(PAGE_SIZE)[None, :]
    masks = jnp.stack([key_mask(jnp.zeros((H_DIFF, tn, PAGE_SIZE), F32)),
                       key_mask(_bias_minus_far(table, rel_last, rel_last > 0))])
    mask_new = key_mask(_bias_minus_far(table, tok - tok.T, tok.T <= tok))
    as_pages = lambda cache: cache.reshape(-1, PAGE_SIZE * H_DIFF, HEAD_W)
    att_s = _sample_attention(q_rows, as_pages(cache_k), as_pages(cache_v), page_table,
                              ksb.reshape(bd, tn * H_DIFF, HEAD_W), vsb.reshape(bd, tn * H_DIFF, D_V),
                              masks, mask_new, lamv, sub, lam_init, tn)
    pad_s = jnp.concatenate([state_conv[l].astype(F32), as_.reshape(bd, tn, d)], axis=1)
    cnv_s = _conv_sample(pad_s, dw, db, lg, lb)
    hs = _mixout(hs, att_s.reshape(bd * tn, d), cnv_s.reshape(bd * tn, d), gas, gcs, w_ao, w_co, w_o)
    hs = _cross(hs, g_cross, wq, cache_mem_k.reshape(bd, n_mem, -1), cache_mem_v.reshape(bd, n_mem, -1), wxo, tn)
    ys = _ffn(hs, g_ffn2, w2g, w2u, w2o, g_fin, True)

    return (yp.reshape(b, t, d), ys.reshape(bd, tn, d),
            kp.reshape(1, b, t, H_DIFF, HEAD_W), vp.reshape(1, b, t, H_DIFF, D_V),
            ap.reshape(b, t, d)[None, :, t - (CONV_W - 1):],
            mkp.reshape(1, b, n_mem, X_HEADS, X_DH), mvp.reshape(1, b, n_mem, X_HEADS, X_DH),
            ks.reshape(1, bd, tn, H_DIFF, HEAD_W), vs.reshape(1, bd, tn, H_DIFF, D_V),
            pad_s[None, :, tn:])
```

```python
import functools
import math

import jax
import jax.numpy as jnp
from jax import lax
from jax.experimental import pallas as pl
from jax.experimental.pallas import tpu as pltpu

F32 = jnp.float32
BF16 = jnp.bfloat16

D_MODEL = 1024
H_DIFF = 8
D_QK = 64
D_V = 128
HEAD_W = 2 * D_QK
CONV_CH = D_MODEL
CONV_W = 31
D_FF = 2816
X_HEADS = 4
X_DH = 128
N_BUCKETS = 32
MAX_DIST = 128
PAGE_SIZE = 128
EPS = 1e-6
NEG = -1e30
QK_SCALE = D_QK ** -0.5
X_SCALE = X_DH ** -0.5
LOG2E = math.log2(math.e)

SUBLANES = 8
V7X_VMEM_BYTES = 64 * 1024 * 1024
VMEM_LIMIT = 56 * 1024 * 1024

ROW_TILE = 256
FFN_TILE = 512
ATT_BLK = 256
Q_BLKS = 2
K_BLKS = 2
HEADS_PER_STEP = 2
CONV_TILE = 256
CONV_HALO = 32
CONV_ROWS = 32
PAGES_PER_STEP = 16


def _params(sem):
    return pltpu.CompilerParams(dimension_semantics=sem, vmem_limit_bytes=VMEM_LIMIT)


def _resident(shape):
    nd = len(shape)
    return pl.BlockSpec(shape, lambda *_: (0,) * nd, pipeline_mode=pl.Buffered(1))


def _rms(x, g):
    return x * lax.rsqrt(jnp.mean(x * x, axis=-1, keepdims=True) + EPS) * g


def _dot(a, b):
    return jnp.dot(a, b, preferred_element_type=F32)


def _dot_nt(a, b):
    return lax.dot_general(a, b, (((1,), (1,)), ((), ())), preferred_element_type=F32)


def _ffn_kernel(x_ref, g_ref, wg_ref, wu_ref, wo_ref, gf_ref, o_ref, *, final_norm):
    x = x_ref[...]
    u = _rms(x, g_ref[...]).astype(BF16)
    gate = _dot(u, wg_ref[...])
    up = _dot(u, wu_ref[...])
    act = (gate * jax.nn.sigmoid(gate) * up).astype(BF16)
    h = x + 0.5 * _dot(act, wo_ref[...])
    if final_norm:
        h = _rms(h, gf_ref[...])
    o_ref[...] = h


def _row_tile(n, rows=ROW_TILE):
    tile = min(rows, n)
    assert n % tile == 0 and tile % 8 == 0
    return tile


def _ffn(x, g, wg, wu, wo, gf, final_norm):
    n = x.shape[0]
    tile = _row_tile(n, FFN_TILE)
    row = pl.BlockSpec((tile, D_MODEL), lambda i: (i, 0))
    return pl.pallas_call(
        functools.partial(_ffn_kernel, final_norm=final_norm),
        out_shape=jax.ShapeDtypeStruct((n, D_MODEL), F32),
        grid=(n // tile,),
        in_specs=[row, _resident(g.shape), _resident(wg.shape), _resident(wu.shape),
                  _resident(wo.shape), _resident(gf.shape)],
        out_specs=row,
        compiler_params=_params(("parallel",)),
        name="ffn",
    )(x, g, wg, wu, wo, gf)


def _mixproj_kernel(h_ref, g_ref, w_ref, wqt_ref, wvt_ref, q_ref, k_ref, v_ref, kb_ref, vb_ref, a_ref, ga_ref,
                    gc_ref, *, transposed):
    u = _rms(h_ref[...], g_ref[...]).astype(BF16)
    d = D_MODEL

    def cols(j):
        return _dot(u, w_ref[:, j * d:(j + 1) * d])

    k = cols(1)
    k_ref[...] = k
    kb_ref[...] = k.astype(BF16)
    v = cols(2)
    v_ref[...] = v
    if transposed:
        q_ref[0] = (_dot_nt(wqt_ref[...], u) * (QK_SCALE * LOG2E)).astype(BF16)
        vb_ref[0] = _dot_nt(wvt_ref[...], u).astype(BF16)
    else:
        q_ref[...] = (cols(0) * (QK_SCALE * LOG2E)).astype(BF16)
        vb_ref[...] = v.astype(BF16)
    a_ref[...] = cols(3) * jax.nn.sigmoid(cols(4))
    ga_ref[...] = jax.nn.sigmoid(cols(5))
    gc_ref[...] = jax.nn.sigmoid(cols(6))


def _mixproj(h, g, w, wqt, wvt, transposed):
    n = h.shape[0]
    tile = ATT_BLK if transposed else _row_tile(n)
    row = pl.BlockSpec((tile, D_MODEL), lambda i: (i, 0))
    f32o = jax.ShapeDtypeStruct((n, D_MODEL), F32)
    b16o = jax.ShapeDtypeStruct((n, D_MODEL), BF16)
    if transposed:
        t16o = jax.ShapeDtypeStruct((n // tile, D_MODEL, tile), BF16)
        trow = pl.BlockSpec((1, D_MODEL, tile), lambda i: (i, 0, 0))
    else:
        t16o, trow = b16o, row
    return pl.pallas_call(
        functools.partial(_mixproj_kernel, transposed=transposed),
        out_shape=(t16o, f32o, f32o, b16o, t16o, f32o, f32o, f32o),
        grid=(n // tile,),
        in_specs=[row, _resident(g.shape), _resident(w.shape), _resident(wqt.shape), _resident(wvt.shape)],
        out_specs=(trow, row, row, row, trow, row, row, row),
        compiler_params=_params(("parallel",)),
        name="mixproj",
    )(h, g, w, wqt, wvt)


def _lambda_value(lamv_ref, lam_init):
    lv = lamv_ref[...]
    d1 = jnp.sum(lv[0:1] * lv[1:2], axis=-1, keepdims=True)
    d2 = jnp.sum(lv[2:3] * lv[3:4], axis=-1, keepdims=True)
    return jnp.exp(d1) - jnp.exp(d2) + lam_init


def _head_out(o1, o2, lam, subln, lam_init):
    o = o1 - lam * o2
    return o * lax.rsqrt(jnp.mean(o * o, axis=-1, keepdims=True) + EPS) * subln * (1.0 - lam_init)


def _pattn_kernel(qt_ref, k_ref, vt_ref, bias_ref, lamv_ref, subln_ref, o_ref, s_ref, p_ref, acc_ref, *,
                  lam_init):
    i = pl.program_id(2)
    blk = ATT_BLK
    nq, nk = Q_BLKS * blk, K_BLKS * blk
    first = (nq // nk) * i
    n_far = jnp.maximum(first - 1, 0)
    n_loop = jnp.maximum(n_far - 1, 0)
    j_near = jnp.maximum(first - 1, 0)
    heads = range(HEADS_PER_STEP)
    lanes = [slice(h * HEAD_W, (h + 1) * HEAD_W) for h in heads]

    def qq_t(h):
        qt = jnp.concatenate([qt_ref[0, c, lanes[h], :] for c in range(Q_BLKS)], axis=1)
        row = lax.broadcasted_iota(jnp.int32, qt.shape, 0)
        zero = jnp.zeros_like(qt)
        return jnp.concatenate([jnp.where(row < D_QK, qt, zero), jnp.where(row >= D_QK, qt, zero)], axis=1)

    qqt = [qq_t(h) for h in heads]

    def scores(h, j, n=nk):
        return _dot(k_ref[0, pl.ds(pl.multiple_of(j * nk, nk), n), lanes[h]], qqt[h])

    def values(h, j, p):
        out = _dot(vt_ref[0, K_BLKS * j, lanes[h], :], p[:blk])
        for c in range(1, p.shape[0] // blk):
            out = out + _dot(vt_ref[0, K_BLKS * j + c, lanes[h], :], p[c * blk:(c + 1) * blk])
        return out

    def softmax(s, m, l):
        m_new = jnp.maximum(m, jnp.max(s, axis=0, keepdims=True))
        alpha = jnp.exp2(m - m_new)
        p = jnp.exp2(s - m_new)
        return m_new, alpha * l + jnp.sum(p, axis=0, keepdims=True), alpha, p.astype(BF16)

    def absent(flag):
        return jnp.where(flag, 2.0 * NEG, 0.0).astype(F32)

    def both_maps(bias):
        return jnp.concatenate([bias, bias], axis=1)

    def head_start(h):
        s_ref[h] = scores(h, 0)
        init = (jnp.full((1, 2 * nq), NEG, F32), jnp.zeros((1, 2 * nq), F32))
        m, l, _, p = softmax(scores(h, first, nq) + both_maps(bias_ref[h, nk:, :]), *init)
        acc = values(h, first, p)
        m, l, alpha, p = softmax(scores(h, j_near) + both_maps(bias_ref[h, 0:nk, :] + absent(first == 0)), m, l)
        acc_ref[h] = alpha * acc + values(h, j_near, p)
        p_ref[h] = jnp.zeros(p_ref.shape[1:], p_ref.dtype)
        return m, l, jnp.ones((1, 2 * nq), F32)

    def far_step(j, carry):
        out = []
        for h in heads:
            m, l, alpha_prev = carry[h]
            acc_ref[h] = alpha_prev * acc_ref[h] + values(h, jnp.maximum(j - 1, 0), p_ref[h])
            m, l, alpha, p = softmax(s_ref[h], m, l)
            p_ref[h] = p
            s_ref[h] = scores(h, j + 1)
            out.append((m, l, alpha))
        return tuple(out)

    def head_finish(h, m, l, alpha_prev):
        acc = alpha_prev * acc_ref[h] + values(h, jnp.maximum(n_loop - 1, 0), p_ref[h])
        m, l, alpha, p = softmax(s_ref[h] + absent(n_far == 0), m, l)
        o = (alpha * acc + values(h, n_loop, p)) / l
        o = o[:, :nq] - _lambda_value(lamv_ref, lam_init) * o[:, nq:]
        o = o * lax.rsqrt(jnp.mean(o * o, axis=0, keepdims=True) + EPS) * subln_ref[...] * (1.0 - lam_init)
        o_ref[0, :, lanes[h]] = o.T.astype(o_ref.dtype)

    carry = lax.fori_loop(0, n_loop, far_step, tuple(head_start(h) for h in heads))
    for h in heads:
        head_finish(h, *carry[h])


def _prompt_attention(qt, kb, vt, bias_tiles, lamv, subln_col, lam_init):
    b, t, _ = kb.shape
    blk = ATT_BLK
    nblk = t // blk
    nq, nk = Q_BLKS * blk, K_BLKS * blk
    hps = HEADS_PER_STEP
    assert nq % nk == 0 and t % nq == 0 and H_DIFF % hps == 0
    return pl.pallas_call(
        functools.partial(_pattn_kernel, lam_init=lam_init),
        out_shape=jax.ShapeDtypeStruct((b, t, H_DIFF * D_V), BF16),
        grid=(H_DIFF // hps, b, nblk // Q_BLKS),
        in_specs=[pl.BlockSpec((1, Q_BLKS, hps * HEAD_W, blk), lambda h, bi, i: (bi, i, h, 0)),
                  pl.BlockSpec((1, t, hps * HEAD_W), lambda h, bi, i: (bi, 0, h)),
                  pl.BlockSpec((1, nblk, hps * D_V, blk), lambda h, bi, i: (bi, 0, h, 0)),
                  pl.BlockSpec((hps, nk + nq, nq), lambda h, bi, i: (h, 0, 0), pipeline_mode=pl.Buffered(1)),
                  pl.BlockSpec(lamv.shape, lambda h, bi, i: (0, 0)),
                  pl.BlockSpec(subln_col.shape, lambda h, bi, i: (0, 0))],
        out_specs=pl.BlockSpec((1, nq, hps * HEAD_W), lambda h, bi, i: (bi, i, h)),
        scratch_shapes=[pltpu.VMEM((hps, nk, 2 * nq), F32), pltpu.VMEM((hps, nk, 2 * nq), BF16),
                        pltpu.VMEM((hps, D_V, 2 * nq), F32)],
        compiler_params=_params(("parallel", "parallel", "arbitrary")),
        name="prompt_attention",
    )(qt, kb, vt, bias_tiles, lamv, subln_col)


def _sattn_kernel(pt_ref, q_ref, *refs, lam_init, n_steps):
    del pt_ref
    npg = PAGES_PER_STEP
    k_refs, v_refs = refs[:npg], refs[npg:2 * npg]
    (kn_ref, vn_ref, masks_ref, mnew_ref, lamv_ref, subln_ref, o_ref, m_ref, l_ref, acc_ref) = refs[2 * npg:]
    c = pl.program_id(1)
    page_rows = k_refs[0].shape[1]

    @pl.when(c == 0)
    def _():
        m_ref[...] = jnp.full(m_ref.shape, NEG, F32)
        l_ref[...] = jnp.zeros(l_ref.shape, F32)
        acc_ref[...] = jnp.zeros(acc_ref.shape, F32)

    q = q_ref[0]
    far_mask = masks_ref[0]
    last_mask = masks_ref[jnp.where(c == n_steps - 1, 1, 0)]
    s_pages = [_dot_nt(q, k_refs[p][0].astype(BF16)) + (far_mask if p < npg - 1 else last_mask)
               for p in range(npg)]
    s = jnp.concatenate(s_pages, axis=1)
    m = m_ref[...]
    m_new = jnp.maximum(m, jnp.max(s, axis=-1, keepdims=True))
    alpha = jnp.exp2(m - m_new)
    p = jnp.exp2(s - m_new)
    l_ref[...] = alpha * l_ref[...] + jnp.sum(p, axis=-1, keepdims=True)
    pb = p.astype(BF16)
    pv = _dot(pb[:, :page_rows], v_refs[0][0].astype(BF16))
    for pg in range(1, npg):
        pv = pv + _dot(pb[:, pg * page_rows:(pg + 1) * page_rows], v_refs[pg][0].astype(BF16))
    acc_ref[...] = alpha * acc_ref[...] + pv
    m_ref[...] = m_new

    @pl.when(c == n_steps - 1)
    def _():
        s_new = _dot_nt(q, kn_ref[0]) + mnew_ref[...]
        m1 = m_ref[...]
        m2 = jnp.maximum(m1, jnp.max(s_new, axis=-1, keepdims=True))
        a2 = jnp.exp2(m1 - m2)
        p2 = jnp.exp2(s_new - m2)
        l2 = a2 * l_ref[...] + jnp.sum(p2, axis=-1, keepdims=True)
        o = (a2 * acc_ref[...] + _dot(p2.astype(BF16), vn_ref[0])) / l2
        lam = _lambda_value(lamv_ref, lam_init)
        rows = o.shape[0] // H_DIFF
        nq = rows // 2
        for h in range(H_DIFF):
            oh = o[h * rows:(h + 1) * rows]
            o_ref[0, :, h * D_V:(h + 1) * D_V] = _head_out(oh[:nq], oh[nq:], lam, subln_ref[...], lam_init
                                                           ).astype(o_ref.dtype)


def _sample_attention(q_rows, cache_k, cache_v, page_table, k_new, v_new, masks, mask_new, lamv, subln,
                      lam_init, nq):
    bd, n_pages = page_table.shape
    npg = PAGES_PER_STEP
    n_steps = n_pages // npg
    rows = q_rows.shape[1]
    page_rows = cache_k.shape[1]

    def page_spec(p):
        return pl.BlockSpec((1, page_rows, HEAD_W), lambda b, c, pt: (pt[b, c * npg + p], 0, 0))

    per_batch = lambda shape: pl.BlockSpec((1,) + shape, lambda b, c, pt: (b, 0, 0))
    const = lambda shape: pl.BlockSpec(shape, lambda b, c, pt: (0,) * len(shape))
    grid_spec = pltpu.PrefetchScalarGridSpec(
        num_scalar_prefetch=1,
        grid=(bd, n_steps),
        in_specs=[per_batch((rows, HEAD_W))]
        + [page_spec(p) for p in range(npg)] * 2
        + [per_batch(k_new.shape[1:]), per_batch(v_new.shape[1:]),
           const(masks.shape), const(mask_new.shape), const(lamv.shape), const(subln.shape)],
        out_specs=per_batch((nq, H_DIFF * D_V)),
        scratch_shapes=[pltpu.VMEM((rows, 1), F32), pltpu.VMEM((rows, 1), F32), pltpu.VMEM((rows, D_V), F32)],
    )
    return pl.pallas_call(
        functools.partial(_sattn_kernel, lam_init=lam_init, n_steps=n_steps),
        out_shape=jax.ShapeDtypeStruct((bd, nq, H_DIFF * D_V), BF16),
        grid_spec=grid_spec,
        compiler_params=_params(("parallel", "arbitrary")),
        name="sample_attention",
    )(page_table, q_rows, *([cache_k] * npg), *([cache_v] * npg), k_new, v_new, masks, mask_new, lamv, subln)


def _ln_silu(y, g, b):
    mu = jnp.mean(y, axis=-1, keepdims=True)
    yc = y - mu
    var = jnp.mean(yc * yc, axis=-1, keepdims=True)
    z = yc * lax.rsqrt(var + EPS) * g + b
    return z * jax.nn.sigmoid(z)


def _conv_prompt_kernel(cur_ref, halo_ref, dw_ref, db_ref, lg_ref, lb_ref, o_ref, win_ref):
    i = pl.program_id(1)
    keep = (i > 0).astype(F32)
    win_ref[0, 0:CONV_HALO, :] = halo_ref[0] * keep
    win_ref[0, CONV_HALO:, :] = cur_ref[0]
    first = CONV_HALO - (CONV_W - 1)
    span = CONV_TILE + max((first + w) // SUBLANES * SUBLANES for w in range(CONV_W) if (first + w) % SUBLANES)
    assert SUBLANES - 1 + span <= CONV_HALO + CONV_TILE
    for s in range(1, SUBLANES):
        win_ref[s, 0:span, :] = win_ref[0, s:s + span, :]

    for r0 in range(0, CONV_TILE, CONV_ROWS):
        acc = None
        for w in range(CONV_W):
            base, s = (first + w) // SUBLANES * SUBLANES, (first + w) % SUBLANES
            term = win_ref[s, r0 + base:r0 + base + CONV_ROWS, :] * dw_ref[w:w + 1, :]
            acc = term if acc is None else acc + term
        y = _ln_silu(acc + db_ref[...], lg_ref[...], lb_ref[...])
        o_ref[0, r0:r0 + CONV_ROWS, :] = y.astype(o_ref.dtype)


def _conv_prompt(a, dw, db, lg, lb):
    b, t, c = a.shape
    per_tile = CONV_TILE // CONV_HALO
    cur = pl.BlockSpec((1, CONV_TILE, c), lambda bi, i: (bi, i, 0))
    halo = pl.BlockSpec((1, CONV_HALO, c), lambda bi, i: (bi, jnp.maximum(i * per_tile - 1, 0), 0))
    return pl.pallas_call(
        _conv_prompt_kernel,
        out_shape=jax.ShapeDtypeStruct((b, t, c), BF16),
        grid=(b, t // CONV_TILE),
        in_specs=[cur, halo, _resident(dw.shape), _resident(db.shape), _resident(lg.shape), _resident(lb.shape)],
        out_specs=cur,
        scratch_shapes=[pltpu.VMEM((SUBLANES, CONV_HALO + CONV_TILE, c), F32)],
        compiler_params=_params(("parallel", "parallel")),
        name="conv_prompt",
    )(a, a, dw, db, lg, lb)


def _conv_sample_kernel(pad_ref, dw_ref, db_ref, lg_ref, lb_ref, o_ref):
    nb, nt = o_ref.shape[0], o_ref.shape[1]
    for bb in range(nb):
        acc = pad_ref[bb, 0:nt, :] * dw_ref[0:1, :]
        for w in range(1, CONV_W):
            acc = acc + pad_ref[bb, w:w + nt, :] * dw_ref[w:w + 1, :]
        y = _ln_silu(acc + db_ref[...], lg_ref[...], lb_ref[...])
        o_ref[bb] = y.astype(o_ref.dtype)


def _conv_sample(pad, dw, db, lg, lb, nb=8):
    bd, tp, c = pad.shape
    nt = tp - (CONV_W - 1)
    return pl.pallas_call(
        _conv_sample_kernel,
        out_shape=jax.ShapeDtypeStruct((bd, nt, c), BF16),
        grid=(bd // nb,),
        in_specs=[pl.BlockSpec((nb, tp, c), lambda i: (i, 0, 0)),
                  _resident(dw.shape), _resident(db.shape), _resident(lg.shape), _resident(lb.shape)],
        out_specs=pl.BlockSpec((nb, nt, c), lambda i: (i, 0, 0)),
        compiler_params=_params(("parallel",)),
        name="conv_sample",
    )(pad, dw, db, lg, lb)


def _mixout_kernel(h_ref, att_ref, cnv_ref, ga_ref, gc_ref, wa_ref, wc_ref, wo_ref, o_ref):
    att = _dot(att_ref[...], wa_ref[...])
    cnv = _dot(cnv_ref[...], wc_ref[...])
    mix = (ga_ref[...] * att + gc_ref[...] * cnv).astype(BF16)
    o_ref[...] = h_ref[...] + _dot(mix, wo_ref[...])


def _mixout(h, att, cnv, ga, gc, wa, wc, wo):
    n = h.shape[0]
    tile = _row_tile(n)
    row = pl.BlockSpec((tile, D_MODEL), lambda i: (i, 0))
    return pl.pallas_call(
        _mixout_kernel,
        out_shape=jax.ShapeDtypeStruct((n, D_MODEL), F32),
        grid=(n // tile,),
        in_specs=[row] * 5 + [_resident(wa.shape), _resident(wc.shape), _resident(wo.shape)],
        out_specs=row,
        compiler_params=_params(("parallel",)),
        name="mixout",
    )(h, att, cnv, ga, gc, wa, wc, wo)


def _memkv_kernel(m_ref, g_ref, w_ref, k_ref, v_ref):
    u = _rms(m_ref[...], g_ref[...]).astype(BF16)
    half = X_HEADS * X_DH
    k_ref[...] = _dot(u, w_ref[:, :half])
    v_ref[...] = _dot(u, w_ref[:, half:])


def _memkv(mem, g, w):
    n = mem.shape[0]
    tile = _row_tile(n)
    half = X_HEADS * X_DH
    out = pl.BlockSpec((tile, half), lambda i: (i, 0))
    o = jax.ShapeDtypeStruct((n, half), F32)
    return pl.pallas_call(
        _memkv_kernel,
        out_shape=(o, o),
        grid=(n // tile,),
        in_specs=[pl.BlockSpec((tile, D_MODEL), lambda i: (i, 0)), _resident(g.shape), _resident(w.shape)],
        out_specs=(out, out),
        compiler_params=_params(("parallel",)),
        name="mem_kv",
    )(mem, g, w)


def _cross_kernel(h_ref, g_ref, wq_ref, mk_ref, mv_ref, wo_ref, o_ref, *, rows_per_batch):
    h = h_ref[...]
    u = _rms(h, g_ref[...]).astype(BF16)
    q = (_dot(u, wq_ref[...]) * X_SCALE).astype(BF16)
    nb, n_mem, width = mk_ref.shape
    mk = mk_ref[...].reshape(nb * n_mem, width).astype(BF16)
    mv = mv_ref[...].reshape(nb * n_mem, width).astype(BF16)
    rows = h.shape[0]
    if nb > 1:
        rb = lax.broadcasted_iota(jnp.int32, (rows, nb * n_mem), 0) // rows_per_batch
        cb = lax.broadcasted_iota(jnp.int32, (rows, nb * n_mem), 1) // n_mem
        own = rb == cb
    outs = []
    for hd in range(X_HEADS):
        sl = slice(hd * X_DH, (hd + 1) * X_DH)
        s = _dot_nt(q[:, sl], mk[:, sl])
        if nb > 1:
            s = jnp.where(own, s, NEG)
        p = jnp.exp(s - jnp.max(s, axis=-1, keepdims=True))
        o = _dot(p.astype(BF16), mv[:, sl]) / jnp.sum(p, axis=-1, keepdims=True)
        outs.append(o.astype(BF16))
    o_ref[...] = h + _dot(jnp.concatenate(outs, axis=1), wo_ref[...])


def _cross(h, g, wq, mk, mv, wo, rows_per_batch):
    n = h.shape[0]
    tile = _row_tile(n)
    if rows_per_batch >= tile:
        nb = 1
        tiles_per_batch = rows_per_batch // tile
        mem_map = lambda i: (i // tiles_per_batch, 0, 0)
    else:
        nb = 8
        tile = nb * rows_per_batch
        mem_map = lambda i: (i, 0, 0)
    row = pl.BlockSpec((tile, D_MODEL), lambda i: (i, 0))
    mem = pl.BlockSpec((nb,) + mk.shape[1:], mem_map)
    return pl.pallas_call(
        functools.partial(_cross_kernel, rows_per_batch=rows_per_batch),
        out_shape=jax.ShapeDtypeStruct((n, D_MODEL), F32),
        grid=(n // tile,),
        in_specs=[row, _resident(g.shape), _resident(wq.shape), mem, mem, _resident(wo.shape)],
        out_specs=row,
        compiler_params=_params(("parallel",)),
        name="cross_attention",
    )(h, g, wq, mk, mv, wo)


def _t5_bucket(rel):
    n = jnp.maximum(rel, 0)
    max_exact = N_BUCKETS // 2
    log_ratio = jnp.log(jnp.maximum(n, 1).astype(F32) / max_exact) / math.log(MAX_DIST / max_exact)
    large = jnp.minimum(max_exact + (log_ratio * (N_BUCKETS - max_exact)).astype(jnp.int32), N_BUCKETS - 1)
    return jnp.where(n < max_exact, n, large)


def _bias_minus_far(table, rel, visible):
    b = jnp.moveaxis(table[_t5_bucket(rel)], -1, 0).astype(F32) - table[N_BUCKETS - 1][:, None, None]
    return jnp.where(visible[None], b, NEG)


def _prompt_bias_tiles(table):
    nq, nk = Q_BLKS * ATT_BLK, K_BLKS * ATT_BLK
    rel = jnp.arange(nq)[None, :] + nk - jnp.arange(nk + nq)[:, None]
    bucket = _t5_bucket(rel)[None]
    by_bucket = (table - table[N_BUCKETS - 1]) * LOG2E
    tile = jnp.zeros((H_DIFF,) + rel.shape, F32)
    for b in range(N_BUCKETS - 1):
        tile = jnp.where(bucket == b, by_bucket[b][:, None, None], tile)
    return jnp.where((rel >= 0)[None], tile, NEG)


def kernel(x_prompt, x_sample, mem_prompt, cache_k, cache_v, page_table, state_conv, cache_mem_k, cache_mem_v,
           rel_bias_table, norm_ffn1, ffn1_w_in, ffn1_w_out, norm_mix, w_in, lambda_q1, lambda_k1, lambda_q2,
           lambda_k2, subln, w_attn_o, conv_dw_w, conv_dw_b, conv_ln_g, conv_ln_b, w_conv_o, w_out, norm_cross,
           norm_mem, w_xq, w_xkv, w_xo, norm_ffn2, ffn2_w_in, ffn2_w_out, norm_final):
    depth = norm_ffn1.shape[0]
    assert depth == 1, "single-layer step"
    l = 0
    lam_init = 0.8 - 0.6 * math.exp(-0.3 * l)
    b, t, d = x_prompt.shape
    bd, tn, _ = x_sample.shape
    n_pages = page_table.shape[1]
    n_mem = mem_prompt.shape[1]
    assert d == D_MODEL and t % (Q_BLKS * ATT_BLK) == 0 and t % CONV_TILE == 0 and ATT_BLK > MAX_DIST
    assert n_pages % PAGES_PER_STEP == 0 and PAGE_SIZE >= MAX_DIST and bd % 8 == 0

    row = lambda v: v.reshape(1, -1).astype(F32)
    bf = lambda w: w.astype(BF16)
    g_ffn1, g_mix, g_cross, g_mem, g_ffn2, g_fin = (row(norm_ffn1[l]), row(norm_mix[l]), row(norm_cross[l]),
                                                    row(norm_mem[l]), row(norm_ffn2[l]), row(norm_final))
    w1g, w1u, w1o = bf(ffn1_w_in[l][:, :D_FF]), bf(ffn1_w_in[l][:, D_FF:]), bf(ffn1_w_out[l])
    w2g, w2u, w2o = bf(ffn2_w_in[l][:, :D_FF]), bf(ffn2_w_in[l][:, D_FF:]), bf(ffn2_w_out[l])
    w_mix, w_ao, w_co, w_o = bf(w_in[l]), bf(w_attn_o[l]), bf(w_conv_o[l]), bf(w_out[l])
    wqt, wvt = w_mix[:, :d].T, w_mix[:, 2 * d:3 * d].T
    wq, wkv, wxo = bf(w_xq[l]), bf(w_xkv[l]), bf(w_xo[l])
    lamv = jnp.stack([lambda_q1[l], lambda_k1[l], lambda_q2[l], lambda_k2[l]]).astype(F32)
    sub = row(subln[l])
    dw, db, lg, lb = conv_dw_w[l].astype(F32), row(conv_dw_b[l]), row(conv_ln_g[l]), row(conv_ln_b[l])
    table = rel_bias_table.astype(F32)

    blk = ATT_BLK
    bias_tiles = _prompt_bias_tiles(table)

    xp = x_prompt.reshape(b * t, d)
    hp = _ffn(xp, g_ffn1, w1g, w1u, w1o, g_fin, False)
    qpt, kp, vp, kpb, vpt, ap, gap, gcp = _mixproj(hp, g_mix, w_mix, wqt, wvt, True)
    att_p = _prompt_attention(qpt.reshape(b, t // blk, d, blk), kpb.reshape(b, t, d), vpt.reshape(b, t // blk, d, blk),
                              bias_tiles, lamv, sub.reshape(-1, 1), lam_init)
    cnv_p = _conv_prompt(ap.reshape(b, t, d), dw, db, lg, lb)
    hp = _mixout(hp, att_p.reshape(b * t, d), cnv_p.reshape(b * t, d), gap, gcp, w_ao, w_co, w_o)
    mkp, mvp = _memkv(mem_prompt.reshape(b * n_mem, d), g_mem, wkv)
    hp = _cross(hp, g_cross, wq, mkp.reshape(b, n_mem, -1), mvp.reshape(b, n_mem, -1), wxo, t)
    yp = _ffn(hp, g_ffn2, w2g, w2u, w2o, g_fin, True)

    xs = x_sample.reshape(bd * tn, d)
    hs = _ffn(xs, g_ffn1, w1g, w1u, w1o, g_fin, False)
    qs, ks, vs, ksb, vsb, as_, gas, gcs = _mixproj(hs, g_mix, w_mix, wqt, wvt, False)
    q5 = jnp.transpose(qs.reshape(bd, tn, H_DIFF, 2, D_QK), (0, 2, 1, 3, 4))[:, :, None]
    own_map = jnp.eye(2, dtype=bool)[None, None, :, None, :, None]
    q_rows = jnp.where(own_map, q5, 0).reshape(bd, H_DIFF * 2 * tn, HEAD_W)
    def key_mask(bias):
        full = jnp.broadcast_to(bias[:, None, :, :, None], (H_DIFF, 2) + bias.shape[1:] + (H_DIFF,))
        own_head = jnp.arange(H_DIFF)[:, None, None, None, None] == jnp.arange(H_DIFF)[None, None, None, None, :]
        return jnp.where(own_head, full * LOG2E, NEG).reshape(H_DIFF * 2 * bias.shape[1], -1)
    tok = jnp.arange(tn)[:, None]
    rel_last = PAGE_SIZE + tok - jnp.arange(PAGE_SIZE)[None, :]
    masks = jnp.stack([key_mask(jnp.zeros((H_DIFF, tn, PAGE_SIZE), F32)),
                       key_mask(_bias_minus_far(table, rel_last, rel_last > 0))])
    mask_new = key_mask(_bias_minus_far(table, tok - tok.T, tok.T <= tok))
    as_pages = lambda cache: cache.reshape(-1, PAGE_SIZE * H_DIFF, HEAD_W)
    att_s = _sample_attention(q_rows, as_pages(cache_k), as_pages(cache_v), page_table,
                              ksb.reshape(bd, tn * H_DIFF, HEAD_W), vsb.reshape(bd, tn * H_DIFF, D_V),
                              masks, mask_new, lamv, sub, lam_init, tn)
    pad_s = jnp.concatenate([state_conv[l].astype(F32), as_.reshape(bd, tn, d)], axis=1)
    cnv_s = _conv_sample(pad_s, dw, db, lg, lb)
    hs = _mixout(hs, att_s.reshape(bd * tn, d), cnv_s.reshape(bd * tn, d), gas, gcs, w_ao, w_co, w_o)
    hs = _cross(hs, g_cross, wq, cache_mem_k.reshape(bd, n_mem, -1), cache_mem_v.reshape(bd, n_mem, -1), wxo, tn)
    ys = _ffn(hs, g_ffn2, w2g, w2u, w2o, g_fin, True)

    return (yp.reshape(b, t, d), ys.reshape(bd, tn, d),
            kp.reshape(1, b, t, H_DIFF, HEAD_W), vp.reshape(1, b, t, H_DIFF, D_V),
            ap.reshape(b, t, d)[None, :, t - (CONV_W - 1):],
            mkp.reshape(1, b, n_mem, X_HEADS, X_DH), mvp.reshape(1, b, n_mem, X_HEADS, X_DH),
            ks.reshape(1, bd, tn, H_DIFF, HEAD_W), vs.reshape(1, bd, tn, H_DIFF, D_V),
            pad_s[None, :, tn:])
```

```python
import functools
import math

import jax
import jax.numpy as jnp
from jax import lax
from jax.experimental import pallas as pl
from jax.experimental.pallas import tpu as pltpu

F32 = jnp.float32
BF16 = jnp.bfloat16

D_MODEL = 1024
H_DIFF = 8
D_QK = 64
D_V = 128
HEAD_W = 2 * D_QK
CONV_CH = D_MODEL
CONV_W = 31
D_FF = 2816
X_HEADS = 4
X_DH = 128
N_BUCKETS = 32
MAX_DIST = 128
PAGE_SIZE = 128
EPS = 1e-6
NEG = -1e30
QK_SCALE = D_QK ** -0.5
X_SCALE = X_DH ** -0.5
LOG2E = math.log2(math.e)

SUBLANES = 8
V7X_VMEM_BYTES = 64 * 1024 * 1024
VMEM_LIMIT = 56 * 1024 * 1024

ROW_TILE = 256
FFN_TILE = 512
ATT_BLK = 256
Q_BLKS = 2
K_BLKS = 1
HEADS_PER_STEP = 4
CONV_TILE = 256
CONV_HALO = 32
CONV_ROWS = 32
PAGES_PER_STEP = 16


def _params(sem):
    return pltpu.CompilerParams(dimension_semantics=sem, vmem_limit_bytes=VMEM_LIMIT)


def _resident(shape):
    nd = len(shape)
    return pl.BlockSpec(shape, lambda *_: (0,) * nd, pipeline_mode=pl.Buffered(1))


def _rms(x, g):
    return x * lax.rsqrt(jnp.mean(x * x, axis=-1, keepdims=True) + EPS) * g


def _dot(a, b):
    return jnp.dot(a, b, preferred_element_type=F32)


def _dot_nt(a, b):
    return lax.dot_general(a, b, (((1,), (1,)), ((), ())), preferred_element_type=F32)


def _ffn_kernel(x_ref, g_ref, wg_ref, wu_ref, wo_ref, gf_ref, o_ref, *, final_norm):
    x = x_ref[...]
    u = _rms(x, g_ref[...]).astype(BF16)
    gate = _dot(u, wg_ref[...])
    up = _dot(u, wu_ref[...])
    act = (gate * jax.nn.sigmoid(gate) * up).astype(BF16)
    h = x + 0.5 * _dot(act, wo_ref[...])
    if final_norm:
        h = _rms(h, gf_ref[...])
    o_ref[...] = h


def _row_tile(n, rows=ROW_TILE):
    tile = min(rows, n)
    assert n % tile == 0 and tile % 8 == 0
    return tile


def _ffn(x, g, wg, wu, wo, gf, final_norm):
    n = x.shape[0]
    tile = _row_tile(n, FFN_TILE)
    row = pl.BlockSpec((tile, D_MODEL), lambda i: (i, 0))
    return pl.pallas_call(
        functools.partial(_ffn_kernel, final_norm=final_norm),
        out_shape=jax.ShapeDtypeStruct((n, D_MODEL), F32),
        grid=(n // tile,),
        in_specs=[row, _resident(g.shape), _resident(wg.shape), _resident(wu.shape),
                  _resident(wo.shape), _resident(gf.shape)],
        out_specs=row,
        compiler_params=_params(("parallel",)),
        name="ffn",
    )(x, g, wg, wu, wo, gf)


def _mixproj_kernel(h_ref, g_ref, w_ref, wqt_ref, wvt_ref, q_ref, k_ref, v_ref, kb_ref, vb_ref, a_ref, ga_ref,
                    gc_ref, *, transposed):
    u = _rms(h_ref[...], g_ref[...]).astype(BF16)
    d = D_MODEL

    def cols(j):
        return _dot(u, w_ref[:, j * d:(j + 1) * d])

    k = cols(1)
    k_ref[...] = k
    kb_ref[...] = k.astype(BF16)
    v = cols(2)
    v_ref[...] = v
    if transposed:
        q_ref[0] = (_dot_nt(wqt_ref[...], u) * (QK_SCALE * LOG2E)).astype(BF16)
        vb_ref[0] = _dot_nt(wvt_ref[...], u).astype(BF16)
    else:
        q_ref[...] = (cols(0) * (QK_SCALE * LOG2E)).astype(BF16)
        vb_ref[...] = v.astype(BF16)
    a_ref[...] = cols(3) * jax.nn.sigmoid(cols(4))
    ga_ref[...] = jax.nn.sigmoid(cols(5))
    gc_ref[...] = jax.nn.sigmoid(cols(6))


def _mixproj(h, g, w, wqt, wvt, transposed):
    n = h.shape[0]
    tile = ATT_BLK if transposed else _row_tile(n)
    row = pl.BlockSpec((tile, D_MODEL), lambda i: (i, 0))
    f32o = jax.ShapeDtypeStruct((n, D_MODEL), F32)
    b16o = jax.ShapeDtypeStruct((n, D_MODEL), BF16)
    if transposed:
        t16o = jax.ShapeDtypeStruct((n // tile, D_MODEL, tile), BF16)
        trow = pl.BlockSpec((1, D_MODEL, tile), lambda i: (i, 0, 0))
    else:
        t16o, trow = b16o, row
    return pl.pallas_call(
        functools.partial(_mixproj_kernel, transposed=transposed),
        out_shape=(t16o, f32o, f32o, b16o, t16o, f32o, f32o, f32o),
        grid=(n // tile,),
        in_specs=[row, _resident(g.shape), _resident(w.shape), _resident(wqt.shape), _resident(wvt.shape)],
        out_specs=(trow, row, row, row, trow, row, row, row),
        compiler_params=_params(("parallel",)),
        name="mixproj",
    )(h, g, w, wqt, wvt)


def _lambda_value(lamv_ref, lam_init):
    lv = lamv_ref[...]
    d1 = jnp.sum(lv[0:1] * lv[1:2], axis=-1, keepdims=True)
    d2 = jnp.sum(lv[2:3] * lv[3:4], axis=-1, keepdims=True)
    return jnp.exp(d1) - jnp.exp(d2) + lam_init


def _head_out(o1, o2, lam, subln, lam_init):
    o = o1 - lam * o2
    return o * lax.rsqrt(jnp.mean(o * o, axis=-1, keepdims=True) + EPS) * subln * (1.0 - lam_init)


def _pattn_kernel(qt_ref, k_ref, vt_ref, bias_ref, lamv_ref, subln_ref, o_ref, s_ref, p_ref, acc_ref, *,
                  lam_init):
    i = pl.program_id(2)
    blk = ATT_BLK
    nq, nk = Q_BLKS * blk, K_BLKS * blk
    first = (nq // nk) * i
    n_far = jnp.maximum(first - 1, 0)
    n_loop = jnp.maximum(n_far - 1, 0)
    j_near = jnp.maximum(first - 1, 0)
    heads = range(HEADS_PER_STEP)
    lanes = [slice(h * HEAD_W, (h + 1) * HEAD_W) for h in heads]

    def qq_t(h):
        qt = jnp.concatenate([qt_ref[0, c, lanes[h], :] for c in range(Q_BLKS)], axis=1)
        row = lax.broadcasted_iota(jnp.int32, qt.shape, 0)
        zero = jnp.zeros_like(qt)
        return jnp.concatenate([jnp.where(row < D_QK, qt, zero), jnp.where(row >= D_QK, qt, zero)], axis=1)

    qqt = [qq_t(h) for h in heads]

    def scores(h, j, n=nk):
        return _dot(k_ref[0, pl.ds(pl.multiple_of(j * nk, nk), n), lanes[h]], qqt[h])

    def values(h, j, p):
        out = _dot(vt_ref[0, K_BLKS * j, lanes[h], :], p[:blk])
        for c in range(1, p.shape[0] // blk):
            out = out + _dot(vt_ref[0, K_BLKS * j + c, lanes[h], :], p[c * blk:(c + 1) * blk])
        return out

    def softmax(s, m, l):
        m_new = jnp.maximum(m, jnp.max(s, axis=0, keepdims=True))
        alpha = jnp.exp2(m - m_new)
        p = jnp.exp2(s - m_new)
        return m_new, alpha * l + jnp.sum(p, axis=0, keepdims=True), alpha, p.astype(BF16)

    def absent(flag):
        return jnp.where(flag, 2.0 * NEG, 0.0).astype(F32)

    def both_maps(bias):
        return jnp.concatenate([bias, bias], axis=1)

    def head_start(h):
        s_ref[h] = scores(h, 0)
        init = (jnp.full((1, 2 * nq), NEG, F32), jnp.zeros((1, 2 * nq), F32))
        m, l, _, p = softmax(scores(h, first, nq) + both_maps(bias_ref[h, nk:, :]), *init)
        acc = values(h, first, p)
        m, l, alpha, p = softmax(scores(h, j_near) + both_maps(bias_ref[h, 0:nk, :] + absent(first == 0)), m, l)
        acc_ref[h] = alpha * acc + values(h, j_near, p)
        p_ref[h] = jnp.zeros(p_ref.shape[1:], p_ref.dtype)
        return m, l, jnp.ones((1, 2 * nq), F32)

    def far_step(j, carry):
        out = []
        for h in heads:
            m, l, alpha_prev = carry[h]
            acc_ref[h] = alpha_prev * acc_ref[h] + values(h, jnp.maximum(j - 1, 0), p_ref[h])
            m, l, alpha, p = softmax(s_ref[h], m, l)
            p_ref[h] = p
            s_ref[h] = scores(h, j + 1)
            out.append((m, l, alpha))
        return tuple(out)

    def head_finish(h, m, l, alpha_prev):
        acc = alpha_prev * acc_ref[h] + values(h, jnp.maximum(n_loop - 1, 0), p_ref[h])
        m, l, alpha, p = softmax(s_ref[h] + absent(n_far == 0), m, l)
        o = (alpha * acc + values(h, n_loop, p)) / l
        o = o[:, :nq] - _lambda_value(lamv_ref, lam_init) * o[:, nq:]
        o = o * lax.rsqrt(jnp.mean(o * o, axis=0, keepdims=True) + EPS) * subln_ref[...] * (1.0 - lam_init)
        o_ref[0, :, lanes[h]] = o.T.astype(o_ref.dtype)

    carry = lax.fori_loop(0, n_loop, far_step, tuple(head_start(h) for h in heads))
    for h in heads:
        head_finish(h, *carry[h])


def _prompt_attention(qt, kb, vt, bias_tiles, lamv, subln_col, lam_init):
    b, t, _ = kb.shape
    blk = ATT_BLK
    nblk = t // blk
    nq, nk = Q_BLKS * blk, K_BLKS * blk
    hps = HEADS_PER_STEP
    assert nq % nk == 0 and t % nq == 0 and H_DIFF % hps == 0
    return pl.pallas_call(
        functools.partial(_pattn_kernel, lam_init=lam_init),
        out_shape=jax.ShapeDtypeStruct((b, t, H_DIFF * D_V), BF16),
        grid=(H_DIFF // hps, b, nblk // Q_BLKS),
        in_specs=[pl.BlockSpec((1, Q_BLKS, hps * HEAD_W, blk), lambda h, bi, i: (bi, i, h, 0)),
                  pl.BlockSpec((1, t, hps * HEAD_W), lambda h, bi, i: (bi, 0, h)),
                  pl.BlockSpec((1, nblk, hps * D_V, blk), lambda h, bi, i: (bi, 0, h, 0)),
                  pl.BlockSpec((hps, nk + nq, nq), lambda h, bi, i: (h, 0, 0), pipeline_mode=pl.Buffered(1)),
                  pl.BlockSpec(lamv.shape, lambda h, bi, i: (0, 0)),
                  pl.BlockSpec(subln_col.shape, lambda h, bi, i: (0, 0))],
        out_specs=pl.BlockSpec((1, nq, hps * HEAD_W), lambda h, bi, i: (bi, i, h)),
        scratch_shapes=[pltpu.VMEM((hps, nk, 2 * nq), F32), pltpu.VMEM((hps, nk, 2 * nq), BF16),
                        pltpu.VMEM((hps, D_V, 2 * nq), F32)],
        compiler_params=_params(("parallel", "parallel", "arbitrary")),
        name="prompt_attention",
    )(qt, kb, vt, bias_tiles, lamv, subln_col)


def _sattn_kernel(pt_ref, q_ref, *refs, lam_init, n_steps):
    del pt_ref
    npg = PAGES_PER_STEP
    k_refs, v_refs = refs[:npg], refs[npg:2 * npg]
    (kn_ref, vn_ref, masks_ref, mnew_ref, lamv_ref, subln_ref, o_ref, m_ref, l_ref, acc_ref) = refs[2 * npg:]
    c = pl.program_id(1)
    page_rows = k_refs[0].shape[1]

    @pl.when(c == 0)
    def _():
        m_ref[...] = jnp.full(m_ref.shape, NEG, F32)
        l_ref[...] = jnp.zeros(l_ref.shape, F32)
        acc_ref[...] = jnp.zeros(acc_ref.shape, F32)

    q = q_ref[0]
    far_mask = masks_ref[0]
    last_mask = masks_ref[jnp.where(c == n_steps - 1, 1, 0)]
    s_pages = [_dot_nt(q, k_refs[p][0].astype(BF16)) + (far_mask if p < npg - 1 else last_mask)
               for p in range(npg)]
    s = jnp.concatenate(s_pages, axis=1)
    m = m_ref[...]
    m_new = jnp.maximum(m, jnp.max(s, axis=-1, keepdims=True))
    alpha = jnp.exp2(m - m_new)
    p = jnp.exp2(s - m_new)
    l_ref[...] = alpha * l_ref[...] + jnp.sum(p, axis=-1, keepdims=True)
    pb = p.astype(BF16)
    pv = _dot(pb[:, :page_rows], v_refs[0][0].astype(BF16))
    for pg in range(1, npg):
        pv = pv + _dot(pb[:, pg * page_rows:(pg + 1) * page_rows], v_refs[pg][0].astype(BF16))
    acc_ref[...] = alpha * acc_ref[...] + pv
    m_ref[...] = m_new

    @pl.when(c == n_steps - 1)
    def _():
        s_new = _dot_nt(q, kn_ref[0]) + mnew_ref[...]
        m1 = m_ref[...]
        m2 = jnp.maximum(m1, jnp.max(s_new, axis=-1, keepdims=True))
        a2 = jnp.exp2(m1 - m2)
        p2 = jnp.exp2(s_new - m2)
        l2 = a2 * l_ref[...] + jnp.sum(p2, axis=-1, keepdims=True)
        o = (a2 * acc_ref[...] + _dot(p2.astype(BF16), vn_ref[0])) / l2
        lam = _lambda_value(lamv_ref, lam_init)
        rows = o.shape[0] // H_DIFF
        nq = rows // 2
        for h in range(H_DIFF):
            oh = o[h * rows:(h + 1) * rows]
            o_ref[0, :, h * D_V:(h + 1) * D_V] = _head_out(oh[:nq], oh[nq:], lam, subln_ref[...], lam_init
                                                           ).astype(o_ref.dtype)


def _sample_attention(q_rows, cache_k, cache_v, page_table, k_new, v_new, masks, mask_new, lamv, subln,
                      lam_init, nq):
    bd, n_pages = page_table.shape
    npg = PAGES_PER_STEP
    n_steps = n_pages // npg
    rows = q_rows.shape[1]
    page_rows = cache_k.shape[1]

    def page_spec(p):
        return pl.BlockSpec((1, page_rows, HEAD_W), lambda b, c, pt: (pt[b, c * npg + p], 0, 0))

    per_batch = lambda shape: pl.BlockSpec((1,) + shape, lambda b, c, pt: (b, 0, 0))
    const = lambda shape: pl.BlockSpec(shape, lambda b, c, pt: (0,) * len(shape))
    grid_spec = pltpu.PrefetchScalarGridSpec(
        num_scalar_prefetch=1,
        grid=(bd, n_steps),
        in_specs=[per_batch((rows, HEAD_W))]
        + [page_spec(p) for p in range(npg)] * 2
        + [per_batch(k_new.shape[1:]), per_batch(v_new.shape[1:]),
           const(masks.shape), const(mask_new.shape), const(lamv.shape), const(subln.shape)],
        out_specs=per_batch((nq, H_DIFF * D_V)),
        scratch_shapes=[pltpu.VMEM((rows, 1), F32), pltpu.VMEM((rows, 1), F32), pltpu.VMEM((rows, D_V), F32)],
    )
    return pl.pallas_call(
        functools.partial(_sattn_kernel, lam_init=lam_init, n_steps=n_steps),
        out_shape=jax.ShapeDtypeStruct((bd, nq, H_DIFF * D_V), BF16),
        grid_spec=grid_spec,
        compiler_params=_params(("parallel", "arbitrary")),
        name="sample_attention",
    )(page_table, q_rows, *([cache_k] * npg), *([cache_v] * npg), k_new, v_new, masks, mask_new, lamv, subln)


def _ln_silu(y, g, b):
    mu = jnp.mean(y, axis=-1, keepdims=True)
    yc = y - mu
    var = jnp.mean(yc * yc, axis=-1, keepdims=True)
    z = yc * lax.rsqrt(var + EPS) * g + b
    return z * jax.nn.sigmoid(z)


def _conv_prompt_kernel(cur_ref, halo_ref, dw_ref, db_ref, lg_ref, lb_ref, o_ref, win_ref):
    i = pl.program_id(1)
    keep = (i > 0).astype(F32)
    win_ref[0, 0:CONV_HALO, :] = halo_ref[0] * keep
    win_ref[0, CONV_HALO:, :] = cur_ref[0]
    first = CONV_HALO - (CONV_W - 1)
    span = CONV_TILE + max((first + w) // SUBLANES * SUBLANES for w in range(CONV_W) if (first + w) % SUBLANES)
    assert SUBLANES - 1 + span <= CONV_HALO + CONV_TILE
    for s in range(1, SUBLANES):
        win_ref[s, 0:span, :] = win_ref[0, s:s + span, :]

    for r0 in range(0, CONV_TILE, CONV_ROWS):
        acc = None
        for w in range(CONV_W):
            base, s = (first + w) // SUBLANES * SUBLANES, (first + w) % SUBLANES
            term = win_ref[s, r0 + base:r0 + base + CONV_ROWS, :] * dw_ref[w:w + 1, :]
            acc = term if acc is None else acc + term
        y = _ln_silu(acc + db_ref[...], lg_ref[...], lb_ref[...])
        o_ref[0, r0:r0 + CONV_ROWS, :] = y.astype(o_ref.dtype)


def _conv_prompt(a, dw, db, lg, lb):
    b, t, c = a.shape
    per_tile = CONV_TILE // CONV_HALO
    cur = pl.BlockSpec((1, CONV_TILE, c), lambda bi, i: (bi, i, 0))
    halo = pl.BlockSpec((1, CONV_HALO, c), lambda bi, i: (bi, jnp.maximum(i * per_tile - 1, 0), 0))
    return pl.pallas_call(
        _conv_prompt_kernel,
        out_shape=jax.ShapeDtypeStruct((b, t, c), BF16),
        grid=(b, t // CONV_TILE),
        in_specs=[cur, halo, _resident(dw.shape), _resident(db.shape), _resident(lg.shape), _resident(lb.shape)],
        out_specs=cur,
        scratch_shapes=[pltpu.VMEM((SUBLANES, CONV_HALO + CONV_TILE, c), F32)],
        compiler_params=_params(("parallel", "parallel")),
        name="conv_prompt",
    )(a, a, dw, db, lg, lb)


def _conv_sample_kernel(pad_ref, dw_ref, db_ref, lg_ref, lb_ref, o_ref):
    nb, nt = o_ref.shape[0], o_ref.shape[1]
    for bb in range(nb):
        acc = pad_ref[bb, 0:nt, :] * dw_ref[0:1, :]
        for w in range(1, CONV_W):
            acc = acc + pad_ref[bb, w:w + nt, :] * dw_ref[w:w + 1, :]
        y = _ln_silu(acc + db_ref[...], lg_ref[...], lb_ref[...])
        o_ref[bb] = y.astype(o_ref.dtype)


def _conv_sample(pad, dw, db, lg, lb, nb=8):
    bd, tp, c = pad.shape
    nt = tp - (CONV_W - 1)
    return pl.pallas_call(
        _conv_sample_kernel,
        out_shape=jax.ShapeDtypeStruct((bd, nt, c), BF16),
        grid=(bd // nb,),
        in_specs=[pl.BlockSpec((nb, tp, c), lambda i: (i, 0, 0)),
                  _resident(dw.shape), _resident(db.shape), _resident(lg.shape), _resident(lb.shape)],
        out_specs=pl.BlockSpec((nb, nt, c), lambda i: (i, 0, 0)),
        compiler_params=_params(("parallel",)),
        name="conv_sample",
    )(pad, dw, db, lg, lb)


def _mixout_kernel(h_ref, att_ref, cnv_ref, ga_ref, gc_ref, wa_ref, wc_ref, wo_ref, o_ref):
    att = _dot(att_ref[...], wa_ref[...])
    cnv = _dot(cnv_ref[...], wc_ref[...])
    mix = (ga_ref[...] * att + gc_ref[...] * cnv).astype(BF16)
    o_ref[...] = h_ref[...] + _dot(mix, wo_ref[...])


def _mixout(h, att, cnv, ga, gc, wa, wc, wo):
    n = h.shape[0]
    tile = _row_tile(n)
    row = pl.BlockSpec((tile, D_MODEL), lambda i: (i, 0))
    return pl.pallas_call(
        _mixout_kernel,
        out_shape=jax.ShapeDtypeStruct((n, D_MODEL), F32),
        grid=(n // tile,),
        in_specs=[row] * 5 + [_resident(wa.shape), _resident(wc.shape), _resident(wo.shape)],
        out_specs=row,
        compiler_params=_params(("parallel",)),
        name="mixout",
    )(h, att, cnv, ga, gc, wa, wc, wo)


def _memkv_kernel(m_ref, g_ref, w_ref, k_ref, v_ref):
    u = _rms(m_ref[...], g_ref[...]).astype(BF16)
    half = X_HEADS * X_DH
    k_ref[...] = _dot(u, w_ref[:, :half])
    v_ref[...] = _dot(u, w_ref[:, half:])


def _memkv(mem, g, w):
    n = mem.shape[0]
    tile = _row_tile(n)
    half = X_HEADS * X_DH
    out = pl.BlockSpec((tile, half), lambda i: (i, 0))
    o = jax.ShapeDtypeStruct((n, half), F32)
    return pl.pallas_call(
        _memkv_kernel,
        out_shape=(o, o),
        grid=(n // tile,),
        in_specs=[pl.BlockSpec((tile, D_MODEL), lambda i: (i, 0)), _resident(g.shape), _resident(w.shape)],
        out_specs=(out, out),
        compiler_params=_params(("parallel",)),
        name="mem_kv",
    )(mem, g, w)


def _cross_kernel(h_ref, g_ref, wq_ref, mk_ref, mv_ref, wo_ref, o_ref, *, rows_per_batch):
    h = h_ref[...]
    u = _rms(h, g_ref[...]).astype(BF16)
    q = (_dot(u, wq_ref[...]) * X_SCALE).astype(BF16)
    nb, n_mem, width = mk_ref.shape
    mk = mk_ref[...].reshape(nb * n_mem, width).astype(BF16)
    mv = mv_ref[...].reshape(nb * n_mem, width).astype(BF16)
    rows = h.shape[0]
    if nb > 1:
        rb = lax.broadcasted_iota(jnp.int32, (rows, nb * n_mem), 0) // rows_per_batch
        cb = lax.broadcasted_iota(jnp.int32, (rows, nb * n_mem), 1) // n_mem
        own = rb == cb
    outs = []
    for hd in range(X_HEADS):
        sl = slice(hd * X_DH, (hd + 1) * X_DH)
        s = _dot_nt(q[:, sl], mk[:, sl])
        if nb > 1:
            s = jnp.where(own, s, NEG)
        p = jnp.exp(s - jnp.max(s, axis=-1, keepdims=True))
        o = _dot(p.astype(BF16), mv[:, sl]) / jnp.sum(p, axis=-1, keepdims=True)
        outs.append(o.astype(BF16))
    o_ref[...] = h + _dot(jnp.concatenate(outs, axis=1), wo_ref[...])


def _cross(h, g, wq, mk, mv, wo, rows_per_batch):
    n = h.shape[0]
    tile = _row_tile(n)
    if rows_per_batch >= tile:
        nb = 1
        tiles_per_batch = rows_per_batch // tile
        mem_map = lambda i: (i // tiles_per_batch, 0, 0)
    else:
        nb = 8
        tile = nb * rows_per_batch
        mem_map = lambda i: (i, 0, 0)
    row = pl.BlockSpec((tile, D_MODEL), lambda i: (i, 0))
    mem = pl.BlockSpec((nb,) + mk.shape[1:], mem_map)
    return pl.pallas_call(
        functools.partial(_cross_kernel, rows_per_batch=rows_per_batch),
        out_shape=jax.ShapeDtypeStruct((n, D_MODEL), F32),
        grid=(n // tile,),
        in_specs=[row, _resident(g.shape), _resident(wq.shape), mem, mem, _resident(wo.shape)],
        out_specs=row,
        compiler_params=_params(("parallel",)),
        name="cross_attention",
    )(h, g, wq, mk, mv, wo)


def _t5_bucket(rel):
    n = jnp.maximum(rel, 0)
    max_exact = N_BUCKETS // 2
    log_ratio = jnp.log(jnp.maximum(n, 1).astype(F32) / max_exact) / math.log(MAX_DIST / max_exact)
    large = jnp.minimum(max_exact + (log_ratio * (N_BUCKETS - max_exact)).astype(jnp.int32), N_BUCKETS - 1)
    return jnp.where(n < max_exact, n, large)


def _bias_minus_far(table, rel, visible):
    b = jnp.moveaxis(table[_t5_bucket(rel)], -1, 0).astype(F32) - table[N_BUCKETS - 1][:, None, None]
    return jnp.where(visible[None], b, NEG)


def _prompt_bias_tiles(table):
    nq, nk = Q_BLKS * ATT_BLK, K_BLKS * ATT_BLK
    rel = jnp.arange(nq)[None, :] + nk - jnp.arange(nk + nq)[:, None]
    bucket = _t5_bucket(rel)[None]
    by_bucket = (table - table[N_BUCKETS - 1]) * LOG2E
    tile = jnp.zeros((H_DIFF,) + rel.shape, F32)
    for b in range(N_BUCKETS - 1):
        tile = jnp.where(bucket == b, by_bucket[b][:, None, None], tile)
    return jnp.where((rel >= 0)[None], tile, NEG)


def kernel(x_prompt, x_sample, mem_prompt, cache_k, cache_v, page_table, state_conv, cache_mem_k, cache_mem_v,
           rel_bias_table, norm_ffn1, ffn1_w_in, ffn1_w_out, norm_mix, w_in, lambda_q1, lambda_k1, lambda_q2,
           lambda_k2, subln, w_attn_o, conv_dw_w, conv_dw_b, conv_ln_g, conv_ln_b, w_conv_o, w_out, norm_cross,
           norm_mem, w_xq, w_xkv, w_xo, norm_ffn2, ffn2_w_in, ffn2_w_out, norm_final):
    depth = norm_ffn1.shape[0]
    assert depth == 1, "single-layer step"
    l = 0
    lam_init = 0.8 - 0.6 * math.exp(-0.3 * l)
    b, t, d = x_prompt.shape
    bd, tn, _ = x_sample.shape
    n_pages = page_table.shape[1]
    n_mem = mem_prompt.shape[1]
    assert d == D_MODEL and t % (Q_BLKS * ATT_BLK) == 0 and t % CONV_TILE == 0 and ATT_BLK > MAX_DIST
    assert n_pages % PAGES_PER_STEP == 0 and PAGE_SIZE >= MAX_DIST and bd % 8 == 0

    row = lambda v: v.reshape(1, -1).astype(F32)
    bf = lambda w: w.astype(BF16)
    g_ffn1, g_mix, g_cross, g_mem, g_ffn2, g_fin = (row(norm_ffn1[l]), row(norm_mix[l]), row(norm_cross[l]),
                                                    row(norm_mem[l]), row(norm_ffn2[l]), row(norm_final))
    w1g, w1u, w1o = bf(ffn1_w_in[l][:, :D_FF]), bf(ffn1_w_in[l][:, D_FF:]), bf(ffn1_w_out[l])
    w2g, w2u, w2o = bf(ffn2_w_in[l][:, :D_FF]), bf(ffn2_w_in[l][:, D_FF:]), bf(ffn2_w_out[l])
    w_mix, w_ao, w_co, w_o = bf(w_in[l]), bf(w_attn_o[l]), bf(w_conv_o[l]), bf(w_out[l])
    wqt, wvt = w_mix[:, :d].T, w_mix[:, 2 * d:3 * d].T
    wq, wkv, wxo = bf(w_xq[l]), bf(w_xkv[l]), bf(w_xo[l])
    lamv = jnp.stack([lambda_q1[l], lambda_k1[l], lambda_q2[l], lambda_k2[l]]).astype(F32)
    sub = row(subln[l])
    dw, db, lg, lb = conv_dw_w[l].astype(F32), row(conv_dw_b[l]), row(conv_ln_g[l]), row(conv_ln_b[l])
    table = rel_bias_table.astype(F32)

    blk = ATT_BLK
    bias_tiles = _prompt_bias_tiles(table)

    xp = x_prompt.reshape(b * t, d)
    hp = _ffn(xp, g_ffn1, w1g, w1u, w1o, g_fin, False)
    qpt, kp, vp, kpb, vpt, ap, gap, gcp = _mixproj(hp, g_mix, w_mix, wqt, wvt, True)
    att_p = _prompt_attention(qpt.reshape(b, t // blk, d, blk), kpb.reshape(b, t, d), vpt.reshape(b, t // blk, d, blk),
                              bias_tiles, lamv, sub.reshape(-1, 1), lam_init)
    cnv_p = _conv_prompt(ap.reshape(b, t, d), dw, db, lg, lb)
    hp = _mixout(hp, att_p.reshape(b * t, d), cnv_p.reshape(b * t, d), gap, gcp, w_ao, w_co, w_o)
    mkp, mvp = _memkv(mem_prompt.reshape(b * n_mem, d), g_mem, wkv)
    hp = _cross(hp, g_cross, wq, mkp.reshape(b, n_mem, -1), mvp.reshape(b, n_mem, -1), wxo, t)
    yp = _ffn(hp, g_ffn2, w2g, w2u, w2o, g_fin, True)

    xs = x_sample.reshape(bd * tn, d)
    hs = _ffn(xs, g_ffn1, w1g, w1u, w1o, g_fin, False)
    qs, ks, vs, ksb, vsb, as_, gas, gcs = _mixproj(hs, g_mix, w_mix, wqt, wvt, False)
    q5 = jnp.transpose(qs.reshape(bd, tn, H_DIFF, 2, D_QK), (0, 2, 1, 3, 4))[:, :, None]
    own_map = jnp.eye(2, dtype=bool)[None, None, :, None, :, None]
    q_rows = jnp.where(own_map, q5, 0).reshape(bd, H_DIFF * 2 * tn, HEAD_W)
    def key_mask(bias):
        full = jnp.broadcast_to(bias[:, None, :, :, None], (H_DIFF, 2) + bias.shape[1:] + (H_DIFF,))
        own_head = jnp.arange(H_DIFF)[:, None, None, None, None] == jnp.arange(H_DIFF)[None, None, None, None, :]
        return jnp.where(own_head, full * LOG2E, NEG).reshape(H_DIFF * 2 * bias.shape[1], -1)
    tok = jnp.arange(tn)[:, None]
    rel_last = PAGE_SIZE + tok - jnp.arange(PAGE_SIZE)[None, :]
    masks = jnp.stack([key_mask(jnp.zeros((H_DIFF, tn, PAGE_SIZE), F32)),
                       key_mask(_bias_minus_far(table, rel_last, rel_last > 0))])
    mask_new = key_mask(_bias_minus_far(table, tok - tok.T, tok.T <= tok))
    as_pages = lambda cache: cache.reshape(-1, PAGE_SIZE * H_DIFF, HEAD_W)
    att_s = _sample_attention(q_rows, as_pages(cache_k), as_pages(cache_v), page_table,
                              ksb.reshape(bd, tn * H_DIFF, HEAD_W), vsb.reshape(bd, tn * H_DIFF, D_V),
                              masks, mask_new, lamv, sub, lam_init, tn)
    pad_s = jnp.concatenate([state_conv[l].astype(F32), as_.reshape(bd, tn, d)], axis=1)
    cnv_s = _conv_sample(pad_s, dw, db, lg, lb)
    hs = _mixout(hs, att_s.reshape(bd * tn, d), cnv_s.reshape(bd * tn, d), gas, gcs, w_ao, w_co, w_o)
    hs = _cross(hs, g_cross, wq, cache_mem_k.reshape(bd, n_mem, -1), cache_mem_v.reshape(bd, n_mem, -1), wxo, tn)
    ys = _ffn(hs, g_ffn2, w2g, w2u, w2o, g_fin, True)

    return (yp.reshape(b, t, d), ys.reshape(bd, tn, d),
            kp.reshape(1, b, t, H_DIFF, HEAD_W), vp.reshape(1, b, t, H_DIFF, D_V),
            ap.reshape(b, t, d)[None, :, t - (CONV_W - 1):],
            mkp.reshape(1, b, n_mem, X_HEADS, X_DH), mvp.reshape(1, b, n_mem, X_HEADS, X_DH),
            ks.reshape(1, bd, tn, H_DIFF, HEAD_W), vs.reshape(1, bd, tn, H_DIFF, D_V),
            pad_s[None, :, tn:])
```
